```python
import math
import jax, jax.numpy as jnp
from jax import lax
import numpy as np

D_MODEL = 1024
BATCH = 8
SEQ = 2048
DEPTH = 2
DEC_BATCH = 32
DEC_SEQ = 2048
PAST_LEN = 128

GRID_W = 64
Q_BLOCK = 128
HEAD_DIM = 64
EPS = 1e-6
A_HEADS = 8
A_KV_HEADS = 2
A_GROUP = A_HEADS // A_KV_HEADS
A_WIDTH = A_HEADS * HEAD_DIM
A_KV_WIDTH = A_KV_HEADS * HEAD_DIM
ROT_FREQS = HEAD_DIM // 4
ROPE_THETA = 10000.0
B_HEADS = 4
B_V_DIM = 2 * HEAD_DIM
B_QK_WIDTH = B_HEADS * 2 * HEAD_DIM
B_WIDTH = B_HEADS * B_V_DIM
N_BRANCHES = 2
IN_COLS = A_WIDTH + 2 * A_KV_WIDTH + 2 * B_QK_WIDTH + B_WIDTH + N_BRANCHES * D_MODEL
REL_BUCKETS = 32
REL_MAX_DIST = 128
N_EXPERTS = 16
N_GROUPS = 4
EXPERTS_PER_GROUP = N_EXPERTS // N_GROUPS
TOP_K = 2
D_FF_EXPERT = 512

kernel_name = 'hybrid_gqa_diffattn_grouped_moe_encoder'


def rms_norm(x, g):
    xf = x.astype(jnp.float32)
    y = xf * lax.rsqrt(jnp.mean(xf * xf, axis=-1, keepdims=True) + EPS)
    return y.astype(x.dtype) * g


def axial_rope_tables(seq_len):
    rows = seq_len // GRID_W
    row = jnp.repeat(jnp.arange(rows, dtype=jnp.float32), GRID_W)
    col = jnp.tile(jnp.arange(GRID_W, dtype=jnp.float32), rows)
    inv = ROPE_THETA ** (-jnp.arange(ROT_FREQS, dtype=jnp.float32) / ROT_FREQS)
    ang = jnp.stack([row[:, None] * inv, col[:, None] * inv], axis=1)
    return jnp.cos(ang), jnp.sin(ang)


def apply_axial_rope(x, cos, sin):
    xf = x.astype(jnp.float32).reshape(x.shape[:-1] + (2, 2, ROT_FREQS))
    x1 = xf[..., 0, :]
    x2 = xf[..., 1, :]
    out = jnp.stack([x1 * cos - x2 * sin, x2 * cos + x1 * sin], axis=-2)
    return out.reshape(x.shape).astype(x.dtype)


def rel_bucket(rel):
    nb = REL_BUCKETS // 2
    max_exact = nb // 2
    ret = jnp.where(rel > 0, nb, 0)
    n = jnp.abs(rel)
    large = max_exact + (jnp.log(jnp.maximum(n, 1).astype(jnp.float32) / max_exact)
                         / math.log(REL_MAX_DIST / max_exact) * (nb - max_exact)).astype(jnp.int32)
    large = jnp.minimum(large, nb - 1)
    return ret + jnp.where(n < max_exact, n, large)


def gqa_attention(q, k, v):
    b, kv, g, s, d = q.shape
    nblk = s // Q_BLOCK
    qb = q.reshape(b, kv, g, nblk, Q_BLOCK, d).transpose(3, 0, 1, 2, 4, 5)
    scale = d ** -0.5

    def body(qblk):
        sc = jnp.einsum('bkgqd,bksd->bkgqs', qblk, k).astype(jnp.float32) * scale
        p = jax.nn.softmax(sc, axis=-1).astype(v.dtype)
        return jnp.einsum('bkgqs,bksd->bkgqd', p, v)

    o = lax.map(body, qb)
    return o.transpose(1, 0, 4, 2, 3, 5).reshape(b, s, kv * g * d)


def diff_attention(q1, q2, k1, k2, v, lam, rel_bias):
    b, h, s, d = q1.shape
    nblk = s // Q_BLOCK
    q1b = q1.reshape(b, h, nblk, Q_BLOCK, d).transpose(2, 0, 1, 3, 4)
    q2b = q2.reshape(b, h, nblk, Q_BLOCK, d).transpose(2, 0, 1, 3, 4)
    kpos = jnp.arange(s)
    scale = d ** -0.5

    def body(args):
        i, a1, a2 = args
        qpos = i * Q_BLOCK + jnp.arange(Q_BLOCK)
        bias = rel_bias[rel_bucket(kpos[None, :] - qpos[:, None])]
        bias = bias.transpose(2, 0, 1).astype(jnp.float32)
        s1 = jnp.einsum('bhqd,bhsd->bhqs', a1, k1).astype(jnp.float32) * scale + bias
        s2 = jnp.einsum('bhqd,bhsd->bhqs', a2, k2).astype(jnp.float32) * scale + bias
        p = jax.nn.softmax(s1, axis=-1) - lam * jax.nn.softmax(s2, axis=-1)
        return jnp.einsum('bhqs,bhsv->bhqv', p.astype(v.dtype), v)

    o = lax.map(body, (jnp.arange(nblk), q1b, q2b))
    return o.transpose(1, 0, 3, 2, 4).reshape(b, s, h, v.shape[-1])


def grouped_moe(h, w_router, router_bias, w_gate, w_up, w_down):
    b, s, dm = h.shape
    t = h.reshape(-1, dm)
    scores = jax.nn.sigmoid((t @ w_router).astype(jnp.float32))
    biased = scores + router_bias.astype(jnp.float32)
    grouped = biased.reshape(-1, N_GROUPS, EXPERTS_PER_GROUP)
    group_score = lax.top_k(grouped, TOP_K)[0].sum(-1)
    sel_group = jnp.argmax(group_score, axis=-1)
    in_group = jnp.take_along_axis(grouped, sel_group[:, None, None], axis=1)[:, 0]
    _, local = lax.top_k(in_group, TOP_K)
    expert_idx = sel_group[:, None] * EXPERTS_PER_GROUP + local
    w = jnp.take_along_axis(scores, expert_idx, axis=-1)
    w = w / jnp.sum(w, axis=-1, keepdims=True)
    gate = jnp.sum(jax.nn.one_hot(expert_idx, N_EXPERTS, dtype=jnp.float32) * w[..., None], axis=1)
    gate = gate.astype(h.dtype)
    out = jnp.zeros_like(t)
    for e in range(N_EXPERTS):
        a = jax.nn.silu(t @ w_gate[e]) * (t @ w_up[e])
        out = out + gate[:, e:e + 1] * (a @ w_down[e])
    return out.reshape(b, s, dm)


def encoder_layer(x, c, layer_idx, cos, sin, w_ada, b_ada, norm_attn, norm_ffn, w_in,
                  q_norm_a, k_norm_a, lambda_q1, lambda_k1, lambda_q2, lambda_k2, subln_b,
                  rel_bias, w_branch_a, w_branch_b, w_out, w_router, router_bias,
                  w_gate, w_up, w_down):
    b, s, _ = x.shape
    mod = jax.nn.silu(c) @ w_ada + b_ada
    shift1, scale1, gate1, shift2, scale2, gate2 = [m[:, None, :] for m in jnp.split(mod, 6, axis=-1)]

    h = rms_norm(x, norm_attn) * (1 + scale1) + shift1
    proj = h @ w_in
    offs = [A_WIDTH, A_WIDTH + A_KV_WIDTH, A_WIDTH + 2 * A_KV_WIDTH,
            A_WIDTH + 2 * A_KV_WIDTH + B_QK_WIDTH, A_WIDTH + 2 * A_KV_WIDTH + 2 * B_QK_WIDTH,
            A_WIDTH + 2 * A_KV_WIDTH + 2 * B_QK_WIDTH + B_WIDTH]
    qa, ka, va, qb, kb, vb, gates = jnp.split(proj, offs, axis=-1)

    qa = apply_axial_rope(rms_norm(qa.reshape(b, s, A_HEADS, HEAD_DIM), q_norm_a).transpose(0, 2, 1, 3), cos, sin)
    ka = apply_axial_rope(rms_norm(ka.reshape(b, s, A_KV_HEADS, HEAD_DIM), k_norm_a).transpose(0, 2, 1, 3), cos, sin)
    va = va.reshape(b, s, A_KV_HEADS, HEAD_DIM).transpose(0, 2, 1, 3)
    qa = qa.reshape(b, A_KV_HEADS, A_GROUP, s, HEAD_DIM)
    ya = gqa_attention(qa, ka, va) @ w_branch_a

    lam_init = 0.8 - 0.6 * math.exp(-0.3 * layer_idx)
    lam = (jnp.exp(jnp.sum(lambda_q1.astype(jnp.float32) * lambda_k1.astype(jnp.float32)))
           - jnp.exp(jnp.sum(lambda_q2.astype(jnp.float32) * lambda_k2.astype(jnp.float32))) + lam_init)
    qb = qb.reshape(b, s, B_HEADS, 2, HEAD_DIM).transpose(0, 2, 3, 1, 4)
    kb = kb.reshape(b, s, B_HEADS, 2, HEAD_DIM).transpose(0, 2, 3, 1, 4)
    vb = vb.reshape(b, s, B_HEADS, B_V_DIM).transpose(0, 2, 1, 3)
    ob = diff_attention(qb[:, :, 0], qb[:, :, 1], kb[:, :, 0], kb[:, :, 1], vb, lam, rel_bias)
    ob = rms_norm(ob, subln_b) * (1.0 - lam_init)
    yb = ob.reshape(b, s, B_WIDTH) @ w_branch_b

    g_a, g_b = jnp.split(jax.nn.sigmoid(gates), N_BRANCHES, axis=-1)
    merged = (g_a * ya + g_b * yb) @ w_out
    x = x + gate1 * merged

    h2 = rms_norm(x, norm_ffn) * (1 + scale2) + shift2
    x = x + gate2 * grouped_moe(h2, w_router, router_bias, w_gate, w_up, w_down)
    return x


def setup_inputs(seed: int = 0) -> dict:
    key = jax.random.key(seed)
    ks = jax.random.split(key, 32)
    f32 = jnp.float32

    def nrm(k, shape, scale):
        return jax.random.normal(k, shape, f32) * scale

    return {
        'x_prompt': nrm(ks[0], (BATCH, SEQ, D_MODEL), 1.0),
        'x_sample': nrm(ks[1], (DEC_BATCH, DEC_SEQ, D_MODEL), 1.0),
        'c_prompt': nrm(ks[2], (BATCH, D_MODEL), 1.0),
        'c_sample': nrm(ks[3], (DEC_BATCH, D_MODEL), 1.0),
        'w_ada': nrm(ks[4], (DEPTH, D_MODEL, 6 * D_MODEL), 0.5 * D_MODEL ** -0.5),
        'b_ada': nrm(ks[5], (DEPTH, 6 * D_MODEL), 0.02),
        'norm_attn': 1.0 + nrm(ks[6], (DEPTH, D_MODEL), 0.02),
        'norm_ffn': 1.0 + nrm(ks[7], (DEPTH, D_MODEL), 0.02),
        'w_in': nrm(ks[8], (DEPTH, D_MODEL, IN_COLS), D_MODEL ** -0.5),
        'q_norm_a': 1.0 + nrm(ks[9], (DEPTH, HEAD_DIM), 0.02),
        'k_norm_a': 1.0 + nrm(ks[10], (DEPTH, HEAD_DIM), 0.02),
        'lambda_q1': nrm(ks[11], (DEPTH, HEAD_DIM), 0.1),
        'lambda_k1': nrm(ks[12], (DEPTH, HEAD_DIM), 0.1),
        'lambda_q2': nrm(ks[13], (DEPTH, HEAD_DIM), 0.1),
        'lambda_k2': nrm(ks[14], (DEPTH, HEAD_DIM), 0.1),
        'subln_b': 1.0 + nrm(ks[15], (DEPTH, B_V_DIM), 0.02),
        'rel_bias': nrm(ks[16], (REL_BUCKETS, B_HEADS), 0.5),
        'w_branch_a': nrm(ks[17], (DEPTH, A_WIDTH, D_MODEL), A_WIDTH ** -0.5),
        'w_branch_b': nrm(ks[18], (DEPTH, B_WIDTH, D_MODEL), B_WIDTH ** -0.5),
        'w_out': nrm(ks[19], (DEPTH, D_MODEL, D_MODEL), D_MODEL ** -0.5),
        'w_router': nrm(ks[20], (D_MODEL, N_EXPERTS), D_MODEL ** -0.5),
        'router_bias': nrm(ks[21], (N_EXPERTS,), 0.01),
        'w_gate': nrm(ks[22], (DEPTH, N_EXPERTS, D_MODEL, D_FF_EXPERT), D_MODEL ** -0.5),
        'w_up': nrm(ks[23], (DEPTH, N_EXPERTS, D_MODEL, D_FF_EXPERT), D_MODEL ** -0.5),
        'w_down': nrm(ks[24], (DEPTH, N_EXPERTS, D_FF_EXPERT, D_MODEL), D_FF_EXPERT ** -0.5),
        'final_norm': 1.0 + nrm(ks[25], (D_MODEL,), 0.02),
    }


def reference(x_prompt, x_sample, c_prompt, c_sample, w_ada, b_ada, norm_attn, norm_ffn, w_in,
              q_norm_a, k_norm_a, lambda_q1, lambda_k1, lambda_q2, lambda_k2, subln_b, rel_bias,
              w_branch_a, w_branch_b, w_out, w_router, router_bias, w_gate, w_up, w_down,
              final_norm):
    def run(x, c):
        cos, sin = axial_rope_tables(x.shape[1])
        for l in range(DEPTH):
            x = encoder_layer(x, c, l, cos, sin, w_ada[l], b_ada[l], norm_attn[l], norm_ffn[l],
                              w_in[l], q_norm_a[l], k_norm_a[l], lambda_q1[l], lambda_k1[l],
                              lambda_q2[l], lambda_k2[l], subln_b[l], rel_bias, w_branch_a[l],
                              w_branch_b[l], w_out[l], w_router, router_bias, w_gate[l],
                              w_up[l], w_down[l])
        return rms_norm(x, final_norm)

    y_prompt = run(x_prompt, c_prompt)
    y_sample = run(x_sample, c_sample)
    return (y_prompt, y_sample)
```

```python
import functools
import math

import numpy as np
import jax
import jax.numpy as jnp
from jax import lax
from jax.experimental import pallas as pl
from jax.experimental.pallas import tpu as pltpu

F32 = jnp.float32
BF16 = jnp.bfloat16

D_MODEL = 1024
GRID_W = 64
HEAD_DIM = 64
EPS = 1e-6
A_HEADS = 8
A_KV_HEADS = 2
A_WIDTH = A_HEADS * HEAD_DIM
A_KV_WIDTH = A_KV_HEADS * HEAD_DIM
ROT_FREQS = HEAD_DIM // 4
ROPE_THETA = 10000.0
B_HEADS = 4
B_V_DIM = 2 * HEAD_DIM
B_QK_WIDTH = B_HEADS * 2 * HEAD_DIM
B_WIDTH = B_HEADS * B_V_DIM
GATE_WIDTH = 2 * D_MODEL
REL_BUCKETS = 32
REL_MAX_DIST = 128
N_EXPERTS = 16
N_GROUPS = 4
EXPERTS_PER_GROUP = N_EXPERTS // N_GROUPS
D_FF_EXPERT = 512
N_MOD = 6

OFF_QA = 0
OFF_KA = OFF_QA + A_WIDTH
OFF_VA = OFF_KA + A_KV_WIDTH
OFF_QB = OFF_VA + A_KV_WIDTH
OFF_KB = OFF_QB + B_QK_WIDTH
OFF_VB = OFF_KB + B_QK_WIDTH
OFF_G = OFF_VB + B_WIDTH
IN_COLS = OFF_G + GATE_WIDTH

LANES = 128
VMEM_LIMIT = 56 * 1024 * 1024

ROW_TILE = 512
Q_TILE = 256
BIAS_CHUNK = 256


def _params(*sem):
    return pltpu.CompilerParams(dimension_semantics=sem, vmem_limit_bytes=VMEM_LIMIT)


def _rms(x, eps=EPS):
    return x * lax.rsqrt(jnp.mean(x * x, axis=-1, keepdims=True) + eps)


def _lane_ids(shape):
    return lax.broadcasted_iota(jnp.int32, shape, len(shape) - 1)


def _ada_kernel(c_ref, w_ref, b_ref, o_ref):
    c = c_ref[...]
    a = (c * jax.nn.sigmoid(c)).astype(BF16)
    o_ref[...] = jnp.dot(a, w_ref[...].astype(BF16), preferred_element_type=F32) + b_ref[...]


def _ada(c, w_ada, b_ada):
    bsz = c.shape[0]
    ncol = w_ada.shape[1] // D_MODEL
    return pl.pallas_call(
        _ada_kernel,
        grid=(ncol,),
        in_specs=[pl.BlockSpec((bsz, D_MODEL), lambda j: (0, 0)),
                  pl.BlockSpec((D_MODEL, D_MODEL), lambda j: (0, j)),
                  pl.BlockSpec((1, D_MODEL), lambda j: (0, j))],
        out_specs=pl.BlockSpec((bsz, D_MODEL), lambda j: (0, j)),
        out_shape=jax.ShapeDtypeStruct((bsz, w_ada.shape[1]), F32),
        compiler_params=_params("arbitrary"),
        name="ada_mod",
    )(c, w_ada, b_ada.reshape(1, -1))


def _bias_kernel(rb_ref, bucket_ref, o_ref):
    bucket = bucket_ref[...]
    accs = [jnp.zeros(bucket.shape, F32) for _ in range(B_HEADS)]
    for b in range(REL_BUCKETS):
        hit = bucket == b
        for h in range(B_HEADS):
            accs[h] = jnp.where(hit, rb_ref[b * B_HEADS + h], accs[h])
    for h in range(B_HEADS):
        o_ref[h] = accs[h]


def _rel_bucket(rel):
    nb = REL_BUCKETS // 2
    max_exact = nb // 2
    ret = jnp.where(rel > 0, nb, 0)
    n = jnp.abs(rel)
    large = max_exact + (jnp.log(jnp.maximum(n, 1).astype(F32) / max_exact)
                         / math.log(REL_MAX_DIST / max_exact) * (nb - max_exact)).astype(jnp.int32)
    large = jnp.minimum(large, nb - 1)
    return ret + jnp.where(n < max_exact, n, large)


def _bias_table(rel_bias, seq, tq):
    nq = seq // tq
    width = seq + (nq - 1) * tq
    r = jnp.arange(tq, dtype=jnp.int32)[:, None]
    c = jnp.arange(width, dtype=jnp.int32)[None, :]
    bucket = _rel_bucket(c - (nq - 1) * tq - r).astype(jnp.int32)
    return pl.pallas_call(
        _bias_kernel,
        grid=(width // BIAS_CHUNK,),
        in_specs=[pl.BlockSpec(memory_space=pltpu.SMEM),
                  pl.BlockSpec((tq, BIAS_CHUNK), lambda j: (0, j))],
        out_specs=pl.BlockSpec((B_HEADS, tq, BIAS_CHUNK), lambda j: (0, 0, j)),
        out_shape=jax.ShapeDtypeStruct((B_HEADS, tq, width), F32),
        compiler_params=_params("arbitrary"),
        name="rel_bias_table",
    )(rel_bias.reshape(-1), bucket)


def _split_dot_ones(sq, bd):
    hi = sq.astype(BF16)
    lo = (sq - hi.astype(F32)).astype(BF16)
    return (jnp.dot(hi, bd, preferred_element_type=F32)
            + jnp.dot(lo, bd, preferred_element_type=F32))


def _head_norm_rope(y, gain, cos, sin, bd):
    ss = _split_dot_ones(y * y, bd)
    yn = y * lax.rsqrt(ss * (1.0 / HEAD_DIM) + EPS) * gain
    lane = _lane_ids(yn.shape)
    partner = jnp.where((lane & ROT_FREQS) == 0,
                        pltpu.roll(yn, LANES - ROT_FREQS, 1),
                        pltpu.roll(yn, ROT_FREQS, 1))
    return yn * cos + partner * sin


def _inproj_kernel(x_ref, shift_ref, scale_ref, na_ref, w_ref, qn_ref, kn_ref, cos_ref, sin_ref,
                   bd_ref, qa_ref, ka_ref, va_ref, qb_ref, kb_ref, vb_ref, g_ref):
    x = x_ref[...]
    h = _rms(x) * na_ref[...] * (1.0 + scale_ref[0]) + shift_ref[0]
    hb = h.astype(BF16)

    def seg(lo, width):
        return jnp.dot(hb, w_ref[:, lo:lo + width], preferred_element_type=F32)

    cos = cos_ref[...]
    sin = sin_ref[...]
    bd = bd_ref[...]
    qk_scale = HEAD_DIM ** -0.5

    for j in range(A_WIDTH // LANES):
        y = seg(OFF_QA + j * LANES, LANES)
        y = _head_norm_rope(y, qn_ref[...], cos, sin, bd)
        qa_ref[:, j * LANES:(j + 1) * LANES] = (y * qk_scale).astype(BF16)

    lane = _lane_ids((x.shape[0], LANES))
    low = lane < HEAD_DIM
    ka = _head_norm_rope(seg(OFF_KA, LANES), kn_ref[...], cos, sin, bd)
    ka_sw = pltpu.roll(ka, HEAD_DIM, 1)
    ka_ref[:, 0:LANES] = jnp.where(low, ka, ka_sw).astype(BF16)
    ka_ref[:, LANES:2 * LANES] = jnp.where(low, ka_sw, ka).astype(BF16)
    va = seg(OFF_VA, LANES)
    va_sw = pltpu.roll(va, HEAD_DIM, 1)
    va_ref[:, 0:LANES] = jnp.where(low, va, va_sw).astype(BF16)
    va_ref[:, LANES:2 * LANES] = jnp.where(low, va_sw, va).astype(BF16)

    qb_ref[...] = (seg(OFF_QB, B_QK_WIDTH) * qk_scale).astype(BF16)
    kb_ref[...] = seg(OFF_KB, B_QK_WIDTH).astype(BF16)
    vb_ref[...] = seg(OFF_VB, B_WIDTH).astype(BF16)
    chunk = 512
    for j in range(GATE_WIDTH // chunk):
        g_ref[:, j * chunk:(j + 1) * chunk] = jax.nn.sigmoid(seg(OFF_G + j * chunk, chunk)).astype(BF16)


def _inproj(x, mod, norm_attn, w_in_b, qn, kn, cos_t, sin_t, bd, seq):
    n = x.shape[0]
    tm = min(ROW_TILE, seq)
    tps = seq // tm

    def row(width):
        return pl.BlockSpec((tm, width), lambda i: (i, 0))

    def full(a):
        return pl.BlockSpec(a.shape, lambda i: (0,) * a.ndim)

    def modspec(k):
        return pl.BlockSpec((1, 1, D_MODEL), lambda i: (N_MOD * (i // tps) + k, 0, 0))

    outs = [(A_WIDTH, BF16), (2 * LANES, BF16), (2 * LANES, BF16), (B_QK_WIDTH, BF16),
            (B_QK_WIDTH, BF16), (B_WIDTH, BF16), (GATE_WIDTH, BF16)]
    return pl.pallas_call(
        _inproj_kernel,
        grid=(n // tm,),
        in_specs=[row(D_MODEL), modspec(0), modspec(1), full(norm_attn), full(w_in_b), full(qn),
                  full(kn),
                  pl.BlockSpec((tm, LANES), lambda i: (i % tps, 0)),
                  pl.BlockSpec((tm, LANES), lambda i: (i % tps, 0)),
                  full(bd)],
        out_specs=[row(w) for w, _ in outs],
        out_shape=[jax.ShapeDtypeStruct((n, w), d) for w, d in outs],
        compiler_params=_params("arbitrary"),
        name="in_proj",
    )(x, mod, mod, norm_attn, w_in_b, qn, kn, cos_t, sin_t, bd)


_NT = (((1,), (1,)), ((), ()))


def _softmax_parts(s):
    m = jnp.max(s, axis=-1, keepdims=True)
    e = jnp.exp(s - m)
    return e, jnp.sum(e, axis=-1, keepdims=True)


def _gqa_kernel(q_ref, k_ref, v_ref, o_ref):
    q = q_ref[...]
    k = k_ref[...]
    v = v_ref[...]
    low = _lane_ids(q.shape) < HEAD_DIM
    zero = jnp.zeros_like(q)
    outs = []
    for qh in (jnp.where(low, q, zero), jnp.where(low, zero, q)):
        s = lax.dot_general(qh, k, _NT, preferred_element_type=F32)
        e, l = _softmax_parts(s)
        o = jnp.dot(e.astype(BF16), v, preferred_element_type=F32)
        outs.append(o * (1.0 / l))
    o_ref[...] = jnp.where(low, outs[0], outs[1]).astype(o_ref.dtype)


def _gqa(qa, ka, va, bsz, seq):
    n = qa.shape[0]
    tq = min(Q_TILE, seq)
    nq = seq // tq
    pairs = A_WIDTH // LANES
    pairs_per_kv = pairs // A_KV_HEADS
    return pl.pallas_call(
        _gqa_kernel,
        grid=(bsz, pairs, nq),
        in_specs=[pl.BlockSpec((tq, LANES), lambda b, i, t: (b * nq + t, i)),
                  pl.BlockSpec((seq, LANES), lambda b, i, t: (b, i // pairs_per_kv)),
                  pl.BlockSpec((seq, LANES), lambda b, i, t: (b, i // pairs_per_kv))],
        out_specs=pl.BlockSpec((tq, LANES), lambda b, i, t: (b * nq + t, i)),
        out_shape=jax.ShapeDtypeStruct((n, A_WIDTH), BF16),
        compiler_params=_params("arbitrary", "arbitrary", "arbitrary"),
        name="gqa_attention",
    )(qa, ka, va)


def _diff_kernel(lam_init, nq, seq, q_ref, k_ref, v_ref, bias_ref, lq1_ref, lk1_ref, lq2_ref,
                 lk2_ref, sub_ref, o_ref):
    tq = q_ref.shape[0]
    t = pl.program_id(2)
    lam = (jnp.exp(jnp.sum(lq1_ref[...] * lk1_ref[...], axis=-1, keepdims=True))
           - jnp.exp(jnp.sum(lq2_ref[...] * lk2_ref[...], axis=-1, keepdims=True)) + lam_init)
    q = q_ref[...]
    k = k_ref[...]
    low = _lane_ids(q.shape) < HEAD_DIM
    zero = jnp.zeros_like(q)
    start = pl.multiple_of((nq - 1 - t) * tq, LANES)
    bias = bias_ref[0, :, pl.ds(start, seq)]
    s1 = lax.dot_general(jnp.where(low, q, zero), k, _NT, preferred_element_type=F32) + bias
    s2 = lax.dot_general(jnp.where(low, zero, q), k, _NT, preferred_element_type=F32) + bias
    e1, l1 = _softmax_parts(s1)
    e2, l2 = _softmax_parts(s2)
    p = e1 * (1.0 / l1) - e2 * (lam / l2)
    o = jnp.dot(p.astype(BF16), v_ref[...], preferred_element_type=F32)
    o_ref[...] = (_rms(o) * sub_ref[...] * (1.0 - lam_init)).astype(o_ref.dtype)


def _diff(qb, kb, vb, bias_t, lq1, lk1, lq2, lk2, subln, lam_init, bsz, seq):
    n = qb.shape[0]
    tq = bias_t.shape[1]
    nq = seq // tq

    def small(a):
        return pl.BlockSpec(a.shape, lambda h, b, t: (0, 0))

    return pl.pallas_call(
        functools.partial(_diff_kernel, lam_init, nq, seq),
        grid=(B_HEADS, bsz, nq),
        in_specs=[pl.BlockSpec((tq, LANES), lambda h, b, t: (b * nq + t, h)),
                  pl.BlockSpec((seq, LANES), lambda h, b, t: (b, h)),
                  pl.BlockSpec((seq, LANES), lambda h, b, t: (b, h)),
                  pl.BlockSpec((1, tq, bias_t.shape[2]), lambda h, b, t: (h, 0, 0)),
                  small(lq1), small(lk1), small(lq2), small(lk2), small(subln)],
        out_specs=pl.BlockSpec((tq, LANES), lambda h, b, t: (b * nq + t, h)),
        out_shape=jax.ShapeDtypeStruct((n, B_WIDTH), BF16),
        compiler_params=_params("arbitrary", "arbitrary", "arbitrary"),
        name="diff_attention",
    )(qb, kb, vb, bias_t, lq1, lk1, lq2, lk2, subln)


def _route(scores, biased):
    epg = EXPERTS_PER_GROUP
    group_score = []
    for g in range(N_GROUPS):
        v = biased[g * epg:(g + 1) * epg]
        best = None
        for a in range(epg):
            for b in range(a + 1, epg):
                pair = v[a] + v[b]
                best = pair if best is None else jnp.maximum(best, pair)
        group_score.append(best)
    sel = jnp.zeros_like(group_score[0], dtype=jnp.int32)
    top = group_score[0]
    for g in range(1, N_GROUPS):
        better = group_score[g] > top
        sel = jnp.where(better, g, sel)
        top = jnp.where(better, group_score[g], top)

    def pick(vals, k):
        out = vals[k]
        for g in range(1, N_GROUPS):
            out = jnp.where(sel == g, vals[g * epg + k], out)
        return out

    bsel = [pick(biased, k) for k in range(epg)]
    ssel = [pick(scores, k) for k in range(epg)]
    neg = jnp.full_like(bsel[0], -jnp.inf)

    def argmax_first(vals):
        idx = jnp.zeros_like(sel)
        top_v = vals[0]
        for k in range(1, epg):
            better = vals[k] > top_v
            idx = jnp.where(better, k, idx)
            top_v = jnp.where(better, vals[k], top_v)
        return idx

    i1 = argmax_first(bsel)
    i2 = argmax_first([jnp.where(i1 == k, neg, bsel[k]) for k in range(epg)])

    def take(vals, idx):
        out = vals[0]
        for k in range(1, epg):
            out = jnp.where(idx == k, vals[k], out)
        return out

    w1 = take(ssel, i1)
    w2 = take(ssel, i2)
    denom = w1 + w2
    w1 = w1 / denom
    w2 = w2 / denom
    gates = []
    zero = jnp.zeros_like(w1)
    for e in range(N_EXPERTS):
        g, k = divmod(e, epg)
        val = jnp.where(i1 == k, w1, jnp.where(i2 == k, w2, zero))
        gates.append(jnp.where(sel == g, val, zero))
    return gates


def _merge_kernel(oa_ref, ob_ref, g_ref, x_ref, gate1_ref, shift_ref, scale_ref, nf_ref, wa_ref,
                  wb_ref, wo_ref, wr_ref, rb_ref, x2_ref, h2_ref, gate_ref):
    ya = jnp.dot(oa_ref[...], wa_ref[...], preferred_element_type=F32)
    yb = jnp.dot(ob_ref[...], wb_ref[...], preferred_element_type=F32)
    m = g_ref[:, 0:D_MODEL].astype(F32) * ya + g_ref[:, D_MODEL:2 * D_MODEL].astype(F32) * yb
    merged = jnp.dot(m.astype(BF16), wo_ref[...], preferred_element_type=F32)
    x2 = x_ref[...] + gate1_ref[0] * merged
    x2_ref[...] = x2
    h2 = _rms(x2) * nf_ref[...] * (1.0 + scale_ref[0]) + shift_ref[0]
    h2_ref[...] = h2.astype(BF16)

    logits = jnp.dot(h2, wr_ref[...], preferred_element_type=F32, precision=lax.Precision.HIGHEST)
    scores_t = jnp.transpose(jax.nn.sigmoid(logits))
    scores = [scores_t[e:e + 1, :] for e in range(N_EXPERTS)]
    biased = [scores[e] + rb_ref[e] for e in range(N_EXPERTS)]
    gates = _route(scores, biased)
    rows = scores_t.shape[1]
    expert_row = lax.broadcasted_iota(jnp.int32, (N_EXPERTS, rows), 0)
    gate_t = jnp.zeros((N_EXPERTS, rows), F32)
    for e in range(N_EXPERTS):
        gate_t = jnp.where(expert_row == e, gates[e], gate_t)
    pad = jnp.zeros((LANES - N_EXPERTS, rows), F32)
    gate_ref[...] = jnp.transpose(jnp.concatenate([gate_t, pad], axis=0))


def _merge(oa, ob, g, x, mod, norm_ffn, wa_b, wb_b, wo_b, wr_pad, router_bias, seq):
    n = x.shape[0]
    tm = min(ROW_TILE, seq)
    tps = seq // tm

    def row(width):
        return pl.BlockSpec((tm, width), lambda i: (i, 0))

    def full(a):
        return pl.BlockSpec(a.shape, lambda i: (0,) * a.ndim)

    def modspec(k):
        return pl.BlockSpec((1, 1, D_MODEL), lambda i: (N_MOD * (i // tps) + k, 0, 0))

    return pl.pallas_call(
        _merge_kernel,
        grid=(n // tm,),
        in_specs=[row(A_WIDTH), row(B_WIDTH), row(GATE_WIDTH), row(D_MODEL), modspec(2),
                  modspec(3), modspec(4), full(norm_ffn), full(wa_b), full(wb_b), full(wo_b),
                  full(wr_pad), pl.BlockSpec(memory_space=pltpu.SMEM)],
        out_specs=[row(D_MODEL), row(D_MODEL), row(LANES)],
        out_shape=[jax.ShapeDtypeStruct((n, D_MODEL), F32),
                   jax.ShapeDtypeStruct((n, D_MODEL), BF16),
                   jax.ShapeDtypeStruct((n, LANES), F32)],
        compiler_params=_params("arbitrary"),
        name="merge_route",
    )(oa, ob, g, x, mod, mod, mod, norm_ffn, wa_b, wb_b, wo_b, wr_pad, router_bias)


def _moe_kernel(final, h_ref, gate_ref, x_ref, gate2_ref, fn_ref, wg_ref, wu_ref, wd_ref, o_ref,
                acc_ref):
    e = pl.program_id(1)

    @pl.when(e == 0)
    def _():
        acc_ref[...] = jnp.zeros_like(acc_ref)

    h = h_ref[...]
    a = jnp.dot(h, wg_ref[0], preferred_element_type=F32)
    u = jnp.dot(h, wu_ref[0], preferred_element_type=F32)
    act = (a * jax.nn.sigmoid(a) * u).astype(BF16)
    y = jnp.dot(act, wd_ref[0], preferred_element_type=F32)
    gate = gate_ref[...]
    col = jnp.sum(jnp.where(_lane_ids(gate.shape) == e, gate, 0.0), axis=-1, keepdims=True)
    acc_ref[...] += col * y

    @pl.when(e == pl.num_programs(1) - 1)
    def _():
        out = x_ref[...] + gate2_ref[0] * acc_ref[...]
        if final:
            out = _rms(out) * fn_ref[...]
        o_ref[...] = out


def _moe(h2, gate, x2, mod, final_norm, wg_b, wu_b, wd_b, seq, final):
    n = x2.shape[0]
    tm = min(ROW_TILE, seq)
    tps = seq // tm
    return pl.pallas_call(
        functools.partial(_moe_kernel, final),
        grid=(n // tm, N_EXPERTS),
        in_specs=[pl.BlockSpec((tm, D_MODEL), lambda i, e: (i, 0)),
                  pl.BlockSpec((tm, LANES), lambda i, e: (i, 0)),
                  pl.BlockSpec((tm, D_MODEL), lambda i, e: (i, 0)),
                  pl.BlockSpec((1, 1, D_MODEL), lambda i, e: (N_MOD * (i // tps) + 5, 0, 0)),
                  pl.BlockSpec(final_norm.shape, lambda i, e: (0, 0)),
                  pl.BlockSpec((1, D_MODEL, D_FF_EXPERT), lambda i, e: (e, 0, 0)),
                  pl.BlockSpec((1, D_MODEL, D_FF_EXPERT), lambda i, e: (e, 0, 0)),
                  pl.BlockSpec((1, D_FF_EXPERT, D_MODEL), lambda i, e: (e, 0, 0))],
        out_specs=pl.BlockSpec((tm, D_MODEL), lambda i, e: (i, 0)),
        out_shape=jax.ShapeDtypeStruct((n, D_MODEL), F32),
        scratch_shapes=[pltpu.VMEM((tm, D_MODEL), F32)],
        compiler_params=_params("arbitrary", "arbitrary"),
        name="experts",
    )(h2, gate, x2, mod, final_norm, wg_b, wu_b, wd_b)


def _rope_tables(seq):
    rows = seq // GRID_W
    row = jnp.repeat(jnp.arange(rows, dtype=F32), GRID_W)
    col = jnp.tile(jnp.arange(GRID_W, dtype=F32), rows)
    inv = ROPE_THETA ** (-jnp.arange(ROT_FREQS, dtype=F32) / ROT_FREQS)
    ang = jnp.stack([row[:, None] * inv, col[:, None] * inv], axis=1)
    cos = jnp.cos(ang)
    sin = jnp.sin(ang)
    cos_h = jnp.concatenate([cos, cos], axis=-1).reshape(seq, HEAD_DIM)
    sin_h = jnp.concatenate([-sin, sin], axis=-1).reshape(seq, HEAD_DIM)
    return jnp.tile(cos_h, (1, 2)), jnp.tile(sin_h, (1, 2))


def _block_ones():
    idx = np.arange(LANES) // HEAD_DIM
    return jnp.asarray((idx[:, None] == idx[None, :]).astype(np.float32), dtype=BF16)


def kernel(x_prompt, x_sample, c_prompt, c_sample, w_ada, b_ada, norm_attn, norm_ffn, w_in,
           q_norm_a, k_norm_a, lambda_q1, lambda_k1, lambda_q2, lambda_k2, subln_b, rel_bias,
           w_branch_a, w_branch_b, w_out, w_router, router_bias, w_gate, w_up, w_down,
           final_norm):
    depth = w_in.shape[0]
    w_in_b = w_in.astype(BF16)
    wa_b = w_branch_a.astype(BF16)
    wb_b = w_branch_b.astype(BF16)
    wo_b = w_out.astype(BF16)
    wg_b = w_gate.astype(BF16)
    wu_b = w_up.astype(BF16)
    wd_b = w_down.astype(BF16)
    wr_pad = jnp.pad(w_router, ((0, 0), (0, LANES - N_EXPERTS)))
    bd = _block_ones()
    fn = final_norm.reshape(1, -1)

    def run(x3, c):
        bsz, seq, _ = x3.shape
        tq = min(Q_TILE, seq)
        cos_t, sin_t = _rope_tables(seq)
        bias_t = _bias_table(rel_bias, seq, tq)
        x = x3.reshape(bsz * seq, D_MODEL)
        for l in range(depth):
            mod = _ada(c, w_ada[l], b_ada[l]).reshape(bsz * N_MOD, 1, D_MODEL)
            qa, ka, va, qb, kb, vb, g = _inproj(
                x, mod, norm_attn[l].reshape(1, -1), w_in_b[l],
                jnp.tile(q_norm_a[l], 2).reshape(1, -1), jnp.tile(k_norm_a[l], 2).reshape(1, -1),
                cos_t, sin_t, bd, seq)
            oa = _gqa(qa, ka, va, bsz, seq)
            lam_init = 0.8 - 0.6 * math.exp(-0.3 * l)
            ob = _diff(qb, kb, vb, bias_t, lambda_q1[l].reshape(1, -1), lambda_k1[l].reshape(1, -1),
                       lambda_q2[l].reshape(1, -1), lambda_k2[l].reshape(1, -1),
                       subln_b[l].reshape(1, -1), lam_init, bsz, seq)
            x2, h2, gate = _merge(oa, ob, g, x, mod, norm_ffn[l].reshape(1, -1), wa_b[l], wb_b[l],
                                  wo_b[l], wr_pad, router_bias, seq)
            x = _moe(h2, gate, x2, mod, fn, wg_b[l], wu_b[l], wd_b[l], seq, l == depth - 1)
        return x.reshape(bsz, seq, D_MODEL)

    return (run(x_prompt, c_prompt), run(x_sample, c_sample))
```

```python
import functools
import math

import numpy as np
import jax
import jax.numpy as jnp
from jax import lax
from jax.experimental import pallas as pl
from jax.experimental.pallas import tpu as pltpu

F32 = jnp.float32
BF16 = jnp.bfloat16

D_MODEL = 1024
GRID_W = 64
HEAD_DIM = 64
EPS = 1e-6
A_HEADS = 8
A_KV_HEADS = 2
A_WIDTH = A_HEADS * HEAD_DIM
A_KV_WIDTH = A_KV_HEADS * HEAD_DIM
ROT_FREQS = HEAD_DIM // 4
ROPE_THETA = 10000.0
B_HEADS = 4
B_V_DIM = 2 * HEAD_DIM
B_QK_WIDTH = B_HEADS * 2 * HEAD_DIM
B_WIDTH = B_HEADS * B_V_DIM
GATE_WIDTH = 2 * D_MODEL
REL_BUCKETS = 32
REL_MAX_DIST = 128
N_EXPERTS = 16
N_GROUPS = 4
EXPERTS_PER_GROUP = N_EXPERTS // N_GROUPS
D_FF_EXPERT = 512
N_MOD = 6
PAIRS_PER_GROUP = EXPERTS_PER_GROUP * (EXPERTS_PER_GROUP - 1) // 2
N_CLASSES = N_GROUPS * PAIRS_PER_GROUP
CLASS_ROWS = 32
ROW_CHUNKS = D_MODEL // 128

OFF_QA = 0
OFF_KA = OFF_QA + A_WIDTH
OFF_VA = OFF_KA + A_KV_WIDTH
OFF_QB = OFF_VA + A_KV_WIDTH
OFF_KB = OFF_QB + B_QK_WIDTH
OFF_VB = OFF_KB + B_QK_WIDTH
OFF_G = OFF_VB + B_WIDTH
IN_COLS = OFF_G + GATE_WIDTH

LOG2E = math.log2(math.e)
QK_SCALE = HEAD_DIM ** -0.5 * LOG2E

LANES = 128
VMEM_LIMIT = 56 * 1024 * 1024

ROW_TILE = 512
Q_TILE = 256
DIFF_HEADS_PER_STEP = 2
BIAS_CHUNK = 256
MOE_TILE = 256
DMA_ROWS = 2048
DMA_UNROLL = 8


def _params(*sem):
    return pltpu.CompilerParams(dimension_semantics=sem, vmem_limit_bytes=VMEM_LIMIT)


def _rms(x, eps=EPS):
    return x * lax.rsqrt(jnp.mean(x * x, axis=-1, keepdims=True) + eps)


def _lane_ids(shape):
    return lax.broadcasted_iota(jnp.int32, shape, len(shape) - 1)


def _ada_kernel(c_ref, w_ref, b_ref, o_ref):
    c = c_ref[...]
    a = (c * jax.nn.sigmoid(c)).astype(BF16)
    o_ref[...] = jnp.dot(a, w_ref[...].astype(BF16), preferred_element_type=F32) + b_ref[...]


def _ada(c, w_ada, b_ada):
    bsz = c.shape[0]
    ncol = w_ada.shape[1] // D_MODEL
    return pl.pallas_call(
        _ada_kernel,
        grid=(ncol,),
        in_specs=[pl.BlockSpec((bsz, D_MODEL), lambda j: (0, 0)),
                  pl.BlockSpec((D_MODEL, D_MODEL), lambda j: (0, j)),
                  pl.BlockSpec((1, D_MODEL), lambda j: (0, j))],
        out_specs=pl.BlockSpec((bsz, D_MODEL), lambda j: (0, j)),
        out_shape=jax.ShapeDtypeStruct((bsz, w_ada.shape[1]), F32),
        compiler_params=_params("arbitrary"),
        name="ada_mod",
    )(c, w_ada, b_ada.reshape(1, -1))


def _bias_kernel(rb_ref, bucket_ref, o_ref):
    bucket = bucket_ref[...]
    accs = [jnp.zeros(bucket.shape, F32) for _ in range(B_HEADS)]
    for b in range(REL_BUCKETS):
        hit = bucket == b
        for h in range(B_HEADS):
            accs[h] = jnp.where(hit, rb_ref[b * B_HEADS + h], accs[h])
    for h in range(B_HEADS):
        o_ref[h] = accs[h] * LOG2E


def _rel_bucket(rel):
    nb = REL_BUCKETS // 2
    max_exact = nb // 2
    ret = jnp.where(rel > 0, nb, 0)
    n = jnp.abs(rel)
    large = max_exact + (jnp.log(jnp.maximum(n, 1).astype(F32) / max_exact)
                         / math.log(REL_MAX_DIST / max_exact) * (nb - max_exact)).astype(jnp.int32)
    large = jnp.minimum(large, nb - 1)
    return ret + jnp.where(n < max_exact, n, large)


def _bias_table(rel_bias, seq, tq):
    nq = seq // tq
    width = seq + (nq - 1) * tq
    r = jnp.arange(tq, dtype=jnp.int32)[:, None]
    c = jnp.arange(width, dtype=jnp.int32)[None, :]
    bucket = _rel_bucket(c - (nq - 1) * tq - r).astype(jnp.int32)
    return pl.pallas_call(
        _bias_kernel,
        grid=(width // BIAS_CHUNK,),
        in_specs=[pl.BlockSpec(memory_space=pltpu.SMEM),
                  pl.BlockSpec((tq, BIAS_CHUNK), lambda j: (0, j))],
        out_specs=pl.BlockSpec((B_HEADS, tq, BIAS_CHUNK), lambda j: (0, 0, j)),
        out_shape=jax.ShapeDtypeStruct((B_HEADS, tq, width), F32),
        compiler_params=_params("arbitrary"),
        name="rel_bias_table",
    )(rel_bias.reshape(-1), bucket)


def _split_dot_ones(sq, bd):
    hi = sq.astype(BF16)
    lo = (sq - hi.astype(F32)).astype(BF16)
    return (jnp.dot(hi, bd, preferred_element_type=F32)
            + jnp.dot(lo, bd, preferred_element_type=F32))


def _head_norm_rope(y, gain, cos, sin, bd):
    ss = _split_dot_ones(y * y, bd)
    yn = y * lax.rsqrt(ss * (1.0 / HEAD_DIM) + EPS) * gain
    lane = _lane_ids(yn.shape)
    partner = jnp.where((lane & ROT_FREQS) == 0,
                        pltpu.roll(yn, LANES - ROT_FREQS, 1),
                        pltpu.roll(yn, ROT_FREQS, 1))
    return yn * cos + partner * sin


def _inproj_kernel(has_delta, *refs):
    if has_delta:
        x_ref, y_ref, gate2_ref = refs[:3]
        refs = refs[3:]
        x = x_ref[...] + gate2_ref[0] * _load_token_major(y_ref)
        xo_ref = refs[-1]
        xo_ref[...] = x
        refs = refs[:-1]
    else:
        x_ref = refs[0]
        refs = refs[1:]
        x = x_ref[...]
    (shift_ref, scale_ref, na_ref, w_ref, qn_ref, kn_ref, cos_ref, sin_ref, bd_ref,
     qa_ref, ka_ref, va_ref, qb_ref, kb_ref, vb_ref, g_ref) = refs
    h = _rms(x) * na_ref[...] * (1.0 + scale_ref[0]) + shift_ref[0]
    hb = h.astype(BF16)

    def seg(lo, width):
        return jnp.dot(hb, w_ref[:, lo:lo + width], preferred_element_type=F32)

    cos = cos_ref[...]
    sin = sin_ref[...]
    bd = bd_ref[...]

    for j in range(A_WIDTH // LANES):
        y = seg(OFF_QA + j * LANES, LANES)
        y = _head_norm_rope(y, qn_ref[...], cos, sin, bd)
        qa_ref[:, j * LANES:(j + 1) * LANES] = (y * QK_SCALE).astype(BF16)

    lane = _lane_ids((x.shape[0], LANES))
    low = lane < HEAD_DIM
    ka = _head_norm_rope(seg(OFF_KA, LANES), kn_ref[...], cos, sin, bd)
    ka_sw = pltpu.roll(ka, HEAD_DIM, 1)
    ka_ref[:, 0:LANES] = jnp.where(low, ka, ka_sw).astype(BF16)
    ka_ref[:, LANES:2 * LANES] = jnp.where(low, ka_sw, ka).astype(BF16)
    va = seg(OFF_VA, LANES)
    va_sw = pltpu.roll(va, HEAD_DIM, 1)
    va_ref[:, 0:LANES] = jnp.where(low, va, 1.0).astype(BF16)
    va_ref[:, LANES:2 * LANES] = jnp.where(low, 1.0, va_sw).astype(BF16)
    va_ref[:, 2 * LANES:3 * LANES] = jnp.where(low, va_sw, 1.0).astype(BF16)
    va_ref[:, 3 * LANES:4 * LANES] = jnp.where(low, 1.0, va).astype(BF16)

    qb_ref[...] = (seg(OFF_QB, B_QK_WIDTH) * QK_SCALE).astype(BF16)
    kb_ref[...] = seg(OFF_KB, B_QK_WIDTH).astype(BF16)
    ones = jnp.ones((x.shape[0], B_V_DIM), BF16)
    for hd in range(B_HEADS):
        vb_ref[:, 2 * hd * B_V_DIM:(2 * hd + 1) * B_V_DIM] = seg(
            OFF_VB + hd * B_V_DIM, B_V_DIM).astype(BF16)
        vb_ref[:, (2 * hd + 1) * B_V_DIM:(2 * hd + 2) * B_V_DIM] = ones
    chunk = 512
    for j in range(GATE_WIDTH // chunk):
        g_ref[:, j * chunk:(j + 1) * chunk] = jax.nn.sigmoid(seg(OFF_G + j * chunk, chunk)).astype(BF16)


def _inproj(x, delta, mod, prev_mod, norm_attn, w_in_b, qn, kn, cos_t, sin_t, bd, seq):
    n = x.shape[0]
    tm = min(ROW_TILE, seq)
    tps = seq // tm
    has_delta = delta is not None

    def row(width):
        return pl.BlockSpec((tm, width), lambda i: (i, 0))

    def full(a):
        return pl.BlockSpec(a.shape, lambda i: (0,) * a.ndim)

    def modspec(k):
        return pl.BlockSpec((1, 1, D_MODEL), lambda i: (N_MOD * (i // tps) + k, 0, 0))

    outs = [(A_WIDTH, BF16), (2 * LANES, BF16), (4 * LANES, BF16), (B_QK_WIDTH, BF16),
            (B_QK_WIDTH, BF16), (2 * B_WIDTH, BF16), (GATE_WIDTH, BF16)]
    args = [x]
    in_specs = [row(D_MODEL)]
    if has_delta:
        args += [delta, prev_mod]
        in_specs += [pl.BlockSpec((tm * ROW_CHUNKS, LANES), lambda i: (i, 0)), modspec(5)]
        outs = outs + [(D_MODEL, F32)]
    args += [mod, mod, norm_attn, w_in_b, qn, kn, cos_t, sin_t, bd]
    in_specs += [modspec(0), modspec(1), full(norm_attn), full(w_in_b), full(qn), full(kn),
                 pl.BlockSpec((tm, LANES), lambda i: (i % tps, 0)),
                 pl.BlockSpec((tm, LANES), lambda i: (i % tps, 0)),
                 full(bd)]
    return pl.pallas_call(
        functools.partial(_inproj_kernel, has_delta),
        grid=(n // tm,),
        in_specs=in_specs,
        out_specs=[row(w) for w, _ in outs],
        out_shape=[jax.ShapeDtypeStruct((n, w), d) for w, d in outs],
        compiler_params=_params("arbitrary"),
        name="in_proj",
    )(*args)


_NT = (((1,), (1,)), ((), ()))


def _exp_scores(s):
    return jnp.exp2(s - jnp.max(s, axis=-1, keepdims=True)).astype(BF16)


def _gqa_kernel(q_ref, k_ref, v_ref, o_ref):
    k = k_ref[...]
    low = _lane_ids((q_ref.shape[0], LANES)) < HEAD_DIM
    for pair in range(q_ref.shape[1] // LANES):
        q = q_ref[:, pair * LANES:(pair + 1) * LANES]
        zero = jnp.zeros_like(q)
        normed = []
        for half, qh in enumerate((jnp.where(low, q, zero), jnp.where(low, zero, q))):
            e = _exp_scores(lax.dot_general(qh, k, _NT, preferred_element_type=F32))
            ol = jnp.dot(e, v_ref[:, half * LANES:(half + 1) * LANES], preferred_element_type=F32)
            normed.append(ol / pltpu.roll(ol, HEAD_DIM, 1))
        o_ref[:, pair * LANES:(pair + 1) * LANES] = jnp.where(low, normed[0], normed[1]).astype(
            o_ref.dtype)


def _gqa(qa, ka, va, bsz, seq):
    n = qa.shape[0]
    tq = min(Q_TILE, seq)
    nq = seq // tq
    group_width = A_WIDTH // A_KV_HEADS
    return pl.pallas_call(
        _gqa_kernel,
        grid=(bsz, A_KV_HEADS, nq),
        in_specs=[pl.BlockSpec((tq, group_width), lambda b, j, t: (b * nq + t, j)),
                  pl.BlockSpec((seq, LANES), lambda b, j, t: (b, j)),
                  pl.BlockSpec((seq, 2 * LANES), lambda b, j, t: (b, j))],
        out_specs=pl.BlockSpec((tq, group_width), lambda b, j, t: (b * nq + t, j)),
        out_shape=jax.ShapeDtypeStruct((n, A_WIDTH), BF16),
        compiler_params=_params("arbitrary", "arbitrary", "arbitrary"),
        name="gqa_attention",
    )(qa, ka, va)


def _diff_kernel(lam_init, nq, seq, q_ref, k_ref, v_ref, bias_ref, lq1_ref, lk1_ref, lq2_ref,
                 lk2_ref, sub_ref, o_ref):
    tq = q_ref.shape[0]
    t = pl.program_id(2)
    lam = (jnp.exp(jnp.sum(lq1_ref[...] * lk1_ref[...], axis=-1, keepdims=True))
           - jnp.exp(jnp.sum(lq2_ref[...] * lk2_ref[...], axis=-1, keepdims=True)) + lam_init)
    low = _lane_ids((tq, LANES)) < HEAD_DIM
    start = pl.multiple_of((nq - 1 - t) * tq, LANES)
    for hd in range(DIFF_HEADS_PER_STEP):
        q = q_ref[:, hd * LANES:(hd + 1) * LANES]
        k = k_ref[:, hd * LANES:(hd + 1) * LANES]
        v = v_ref[:, 2 * hd * B_V_DIM:(2 * hd + 2) * B_V_DIM]
        zero = jnp.zeros_like(q)
        bias = bias_ref[hd, :, pl.ds(start, seq)]
        parts = []
        for qh in (jnp.where(low, q, zero), jnp.where(low, zero, q)):
            e = _exp_scores(lax.dot_general(qh, k, _NT, preferred_element_type=F32) + bias)
            ol = jnp.dot(e, v, preferred_element_type=F32)
            parts.append(ol[:, 0:B_V_DIM] / ol[:, B_V_DIM:2 * B_V_DIM])
        o = parts[0] - lam * parts[1]
        o_ref[:, hd * B_V_DIM:(hd + 1) * B_V_DIM] = (
            _rms(o) * sub_ref[...] * (1.0 - lam_init)).astype(o_ref.dtype)


def _diff(qb, kb, vb, bias_t, lq1, lk1, lq2, lk2, subln, lam_init, bsz, seq):
    n = qb.shape[0]
    tq = bias_t.shape[1]
    nq = seq // tq
    hps = DIFF_HEADS_PER_STEP

    def small(a):
        return pl.BlockSpec(a.shape, lambda h, b, t: (0, 0))

    return pl.pallas_call(
        functools.partial(_diff_kernel, lam_init, nq, seq),
        grid=(B_HEADS // hps, bsz, nq),
        in_specs=[pl.BlockSpec((tq, hps * LANES), lambda h, b, t: (b * nq + t, h)),
                  pl.BlockSpec((seq, hps * LANES), lambda h, b, t: (b, h)),
                  pl.BlockSpec((seq, hps * 2 * B_V_DIM), lambda h, b, t: (b, h)),
                  pl.BlockSpec((hps, tq, bias_t.shape[2]), lambda h, b, t: (h, 0, 0),
                               pipeline_mode=pl.Buffered(1)),
                  small(lq1), small(lk1), small(lq2), small(lk2), small(subln)],
        out_specs=pl.BlockSpec((tq, hps * B_V_DIM), lambda h, b, t: (b * nq + t, h)),
        out_shape=jax.ShapeDtypeStruct((n, B_WIDTH), BF16),
        compiler_params=_params("arbitrary", "arbitrary", "arbitrary"),
        name="diff_attention",
    )(qb, kb, vb, bias_t, lq1, lk1, lq2, lk2, subln)


def _load_token_major(ref):
    rows = ref.shape[0] // ROW_CHUNKS
    return jnp.concatenate([ref[pl.ds(c, rows, stride=ROW_CHUNKS), :] for c in range(ROW_CHUNKS)],
                           axis=-1)


def _store_token_major(ref, val):
    rows = val.shape[0]
    for c in range(ROW_CHUNKS):
        ref[pl.ds(c, rows, stride=ROW_CHUNKS), :] = val[:, c * LANES:(c + 1) * LANES]


def _route(biased):
    epg = EXPERTS_PER_GROUP
    group_score = []
    for g in range(N_GROUPS):
        v = biased[g * epg:(g + 1) * epg]
        best = None
        for a in range(epg):
            for b in range(a + 1, epg):
                pair = v[a] + v[b]
                best = pair if best is None else jnp.maximum(best, pair)
        group_score.append(best)
    sel = jnp.zeros_like(group_score[0], dtype=jnp.int32)
    top = group_score[0]
    for g in range(1, N_GROUPS):
        better = group_score[g] > top
        sel = jnp.where(better, g, sel)
        top = jnp.where(better, group_score[g], top)

    def pick(vals, k):
        out = vals[k]
        for g in range(1, N_GROUPS):
            out = jnp.where(sel == g, vals[g * epg + k], out)
        return out

    bsel = [pick(biased, k) for k in range(epg)]
    neg = jnp.full_like(bsel[0], -jnp.inf)

    def argmax_first(vals):
        idx = jnp.zeros_like(sel)
        top_v = vals[0]
        for k in range(1, epg):
            better = vals[k] > top_v
            idx = jnp.where(better, k, idx)
            top_v = jnp.where(better, vals[k], top_v)
        return idx

    i1 = argmax_first(bsel)
    i2 = argmax_first([jnp.where(i1 == k, neg, bsel[k]) for k in range(epg)])

    k_lo = jnp.minimum(i1, i2)
    k_hi = jnp.maximum(i1, i2)
    pair = jnp.where(k_lo == 0, k_hi - 1, jnp.where(k_lo == 1, k_hi + 1, PAIRS_PER_GROUP - 1))
    return sel * PAIRS_PER_GROUP + pair


def _merge_kernel(oa_ref, ob_ref, g_ref, x_ref, gate1_ref, shift_ref, scale_ref, nf_ref, wa_ref,
                  wb_ref, wo_ref, wr_ref, tri_ref, rb_ref, x2_ref, hx_ref, cls_ref, rank_ref,
                  count_ref, carry_ref):
    @pl.when(pl.program_id(0) == 0)
    def _():
        carry_ref[...] = jnp.zeros_like(carry_ref)

    ya = jnp.dot(oa_ref[...], wa_ref[...], preferred_element_type=F32)
    yb = jnp.dot(ob_ref[...], wb_ref[...], preferred_element_type=F32)
    m = g_ref[:, 0:D_MODEL].astype(F32) * ya + g_ref[:, D_MODEL:2 * D_MODEL].astype(F32) * yb
    merged = jnp.dot(m.astype(BF16), wo_ref[...], preferred_element_type=F32)
    x2 = x_ref[...] + gate1_ref[0] * merged
    x2_ref[...] = x2
    h2 = _rms(x2) * nf_ref[...] * (1.0 + scale_ref[0]) + shift_ref[0]
    _store_token_major(hx_ref, h2)

    logits = jnp.dot(h2, wr_ref[...], preferred_element_type=F32, precision=lax.Precision.HIGHEST)
    scores_t = jnp.transpose(jax.nn.sigmoid(logits))
    biased = [scores_t[e:e + 1, :] + rb_ref[e] for e in range(N_EXPERTS)]
    cls = _route(biased)
    rows = scores_t.shape[1]

    class_row = lax.broadcasted_iota(jnp.int32, (CLASS_ROWS, rows), 0)
    onehot = class_row == cls
    before = jnp.dot(jnp.where(onehot, 1.0, 0.0).astype(BF16), tri_ref[...],
                     preferred_element_type=F32)
    carry = carry_ref[...]
    rank = jnp.sum(jnp.where(onehot, before + carry[:, 0:1], 0.0), axis=0, keepdims=True)
    carry = carry + jnp.sum(jnp.where(onehot, 1.0, 0.0), axis=1, keepdims=True)
    carry_ref[...] = carry
    count_ref[...] = carry
    cls_ref[0] = cls
    rank_ref[0] = rank.astype(jnp.int32)


def _merge(oa, ob, g, x, mod, norm_ffn, wa_b, wb_b, wo_b, wr_pad, tri, router_bias, seq):
    n = x.shape[0]
    tm = min(ROW_TILE, seq)
    tps = seq // tm
    nt = n // tm

    def row(width):
        return pl.BlockSpec((tm, width), lambda i: (i, 0))

    def full(a):
        return pl.BlockSpec(a.shape, lambda i: (0,) * a.ndim)

    def modspec(k):
        return pl.BlockSpec((1, 1, D_MODEL), lambda i: (N_MOD * (i // tps) + k, 0, 0))

    lane_vec = pl.BlockSpec((1, 1, tm), lambda i: (i, 0, 0))
    return pl.pallas_call(
        _merge_kernel,
        grid=(nt,),
        in_specs=[row(A_WIDTH), row(B_WIDTH), row(GATE_WIDTH), row(D_MODEL), modspec(2),
                  modspec(3), modspec(4), full(norm_ffn), full(wa_b), full(wb_b), full(wo_b),
                  full(wr_pad), full(tri), pl.BlockSpec(memory_space=pltpu.SMEM)],
        out_specs=[row(D_MODEL), pl.BlockSpec((tm * ROW_CHUNKS, LANES), lambda i: (i, 0)),
                   lane_vec, lane_vec, pl.BlockSpec((CLASS_ROWS, LANES), lambda i: (0, 0))],
        out_shape=[jax.ShapeDtypeStruct((n, D_MODEL), F32),
                   jax.ShapeDtypeStruct((n * ROW_CHUNKS, LANES), F32),
                   jax.ShapeDtypeStruct((nt, 1, tm), jnp.int32),
                   jax.ShapeDtypeStruct((nt, 1, tm), jnp.int32),
                   jax.ShapeDtypeStruct((CLASS_ROWS, LANES), F32)],
        scratch_shapes=[pltpu.VMEM((CLASS_ROWS, LANES), F32)],
        compiler_params=_params("arbitrary"),
        name="merge_route",
    )(oa, ob, g, x, mod, mod, mod, norm_ffn, wa_b, wb_b, wo_b, wr_pad, tri, router_bias)


def _row_copy(src_ref, src_row, dst_ref, dst_row, sem):
    return pltpu.make_async_copy(src_ref.at[src_row], dst_ref.at[dst_row], sem)


def _permute_rows(rows, to_slot, slot_ref, src_ref, dst_ref, sem):
    base = pl.program_id(0) * rows

    def body(j, carry):
        first = j * DMA_UNROLL
        slots = [slot_ref[first + k] for k in range(DMA_UNROLL)]
        for k in range(DMA_UNROLL):
            token = base + first + k
            if to_slot:
                _row_copy(src_ref, token, dst_ref, slots[k], sem).start()
            else:
                _row_copy(src_ref, slots[k], dst_ref, token, sem).start()
        return carry

    lax.fori_loop(0, rows // DMA_UNROLL, body, 0)
    pltpu.make_async_copy(src_ref.at[pl.ds(0, rows)], dst_ref.at[pl.ds(0, rows)], sem).wait()


def _dispatch_kernel(rows, slot_ref, src_ref, init_ref, dst_ref, sem):
    del init_ref
    _permute_rows(rows, True, slot_ref, src_ref, dst_ref, sem)


def _dispatch(hx, slot, n_slots):
    n = hx.shape[0]
    rows = min(DMA_ROWS, n)
    hbm = pl.BlockSpec(memory_space=pl.ANY)
    return pl.pallas_call(
        functools.partial(_dispatch_kernel, rows),
        grid=(n // rows,),
        in_specs=[pl.BlockSpec((rows,), lambda i: (i,), memory_space=pltpu.SMEM), hbm, hbm],
        out_specs=hbm,
        out_shape=jax.ShapeDtypeStruct((n_slots,) + hx.shape[1:], hx.dtype),
        scratch_shapes=[pltpu.SemaphoreType.DMA(())],
        input_output_aliases={2: 0},
        compiler_params=_params("arbitrary"),
        name="dispatch_rows",
    )(slot, hx, jnp.zeros((n_slots,) + hx.shape[1:], hx.dtype))


def _combine_kernel(rows, slot_ref, src_ref, dst_ref, sem):
    _permute_rows(rows, False, slot_ref, src_ref, dst_ref, sem)


def _combine(ys, slot, n):
    rows = min(DMA_ROWS, n)
    hbm = pl.BlockSpec(memory_space=pl.ANY)
    return pl.pallas_call(
        functools.partial(_combine_kernel, rows),
        grid=(n // rows,),
        in_specs=[pl.BlockSpec((rows,), lambda i: (i,), memory_space=pltpu.SMEM), hbm],
        out_specs=hbm,
        out_shape=jax.ShapeDtypeStruct((n,) + ys.shape[1:], ys.dtype),
        scratch_shapes=[pltpu.SemaphoreType.DMA(())],
        compiler_params=_params("arbitrary"),
        name="combine_rows",
    )(slot, ys)


def _moe_kernel(lo_ref, hi_ref, valid_ref, hs_ref, wr_ref, wg_lo, wu_lo, wd_lo, wg_hi, wu_hi,
                wd_hi, o_ref):
    t = pl.program_id(0)
    valid = valid_ref[t] > 0

    @pl.when(jnp.logical_not(valid))
    def _():
        o_ref[...] = jnp.zeros_like(o_ref)

    @pl.when(valid)
    def _():
        h32 = _load_token_major(hs_ref)
        h = h32.astype(BF16)

        def ffn(wg_ref, wu_ref, wd_ref):
            a = jnp.dot(h, wg_ref[0], preferred_element_type=F32)
            u = jnp.dot(h, wu_ref[0], preferred_element_type=F32)
            act = (a * jax.nn.sigmoid(a) * u).astype(BF16)
            return jnp.dot(act, wd_ref[0], preferred_element_type=F32)

        def score(e):
            return jax.nn.sigmoid(jnp.sum(h32 * wr_ref[pl.ds(e, 1), :], axis=-1, keepdims=True))

        s_lo = score(lo_ref[t])
        s_hi = score(hi_ref[t])
        denom = s_lo + s_hi
        y = ((s_lo / denom) * ffn(wg_lo, wu_lo, wd_lo)
             + (s_hi / denom) * ffn(wg_hi, wu_hi, wd_hi))
        _store_token_major(o_ref, y)


def _moe(hs, tile_lo, tile_hi, tile_valid, wr_t, wg_b, wu_b, wd_b):
    n_tiles = tile_lo.shape[0]
    up = (1, D_MODEL, D_FF_EXPERT)
    down = (1, D_FF_EXPERT, D_MODEL)

    def by(sel):
        return lambda t, lo, hi, valid: ((lo, hi)[sel][t], 0, 0)

    rows = pl.BlockSpec((MOE_TILE * ROW_CHUNKS, LANES), lambda t, lo, hi, valid: (t, 0))
    return pl.pallas_call(
        _moe_kernel,
        grid_spec=pltpu.PrefetchScalarGridSpec(
            num_scalar_prefetch=3,
            grid=(n_tiles,),
            in_specs=[rows, pl.BlockSpec(wr_t.shape, lambda t, lo, hi, valid: (0, 0)),
                      pl.BlockSpec(up, by(0)), pl.BlockSpec(up, by(0)), pl.BlockSpec(down, by(0)),
                      pl.BlockSpec(up, by(1)), pl.BlockSpec(up, by(1)), pl.BlockSpec(down, by(1))],
            out_specs=rows),
        out_shape=jax.ShapeDtypeStruct(hs.shape, F32),
        compiler_params=_params("arbitrary"),
        name="experts",
    )(tile_lo, tile_hi, tile_valid, hs, wr_t, wg_b, wu_b, wd_b, wg_b, wu_b, wd_b)


def _tile_plan(counts, n):
    n_tiles = n // MOE_TILE + N_CLASSES
    tiles_c = (counts + MOE_TILE - 1) // MOE_TILE
    ends = jnp.cumsum(tiles_c)
    starts = (ends - tiles_c) * MOE_TILE
    total = ends[-1]
    t = jnp.arange(n_tiles, dtype=jnp.int32)
    last = jnp.maximum(total - 1, 0)
    tile_cls = jnp.sum((jnp.minimum(t, last)[:, None] >= ends[None, :]).astype(jnp.int32), axis=1)
    tile_cls = jnp.minimum(tile_cls, N_CLASSES - 1)
    k_lo, k_hi = np.triu_indices(EXPERTS_PER_GROUP, 1)
    group = np.arange(N_CLASSES) // PAIRS_PER_GROUP
    lo_table = jnp.asarray(group * EXPERTS_PER_GROUP + np.tile(k_lo, N_GROUPS), jnp.int32)
    hi_table = jnp.asarray(group * EXPERTS_PER_GROUP + np.tile(k_hi, N_GROUPS), jnp.int32)
    return (starts.astype(jnp.int32), lo_table[tile_cls], hi_table[tile_cls],
            (t < total).astype(jnp.int32), n_tiles)


def _final_kernel(x_ref, y_ref, gate2_ref, fn_ref, o_ref):
    o_ref[...] = _rms(x_ref[...] + gate2_ref[0] * _load_token_major(y_ref)) * fn_ref[...]


def _final(x2, y, mod, final_norm, seq):
    n = x2.shape[0]
    tm = min(ROW_TILE, seq)
    tps = seq // tm
    row = pl.BlockSpec((tm, D_MODEL), lambda i: (i, 0))
    return pl.pallas_call(
        _final_kernel,
        grid=(n // tm,),
        in_specs=[row, pl.BlockSpec((tm * ROW_CHUNKS, LANES), lambda i: (i, 0)),
                  pl.BlockSpec((1, 1, D_MODEL), lambda i: (N_MOD * (i // tps) + 5, 0, 0)),
                  pl.BlockSpec(final_norm.shape, lambda i: (0, 0))],
        out_specs=row,
        out_shape=jax.ShapeDtypeStruct((n, D_MODEL), F32),
        compiler_params=_params("arbitrary"),
        name="final_norm",
    )(x2, y, mod, final_norm)


def _rope_tables(seq):
    rows = seq // GRID_W
    row = jnp.repeat(jnp.arange(rows, dtype=F32), GRID_W)
    col = jnp.tile(jnp.arange(GRID_W, dtype=F32), rows)
    inv = ROPE_THETA ** (-jnp.arange(ROT_FREQS, dtype=F32) / ROT_FREQS)
    ang = jnp.stack([row[:, None] * inv, col[:, None] * inv], axis=1)
    cos = jnp.cos(ang)
    sin = jnp.sin(ang)
    cos_h = jnp.concatenate([cos, cos], axis=-1).reshape(seq, HEAD_DIM)
    sin_h = jnp.concatenate([-sin, sin], axis=-1).reshape(seq, HEAD_DIM)
    return jnp.tile(cos_h, (1, 2)), jnp.tile(sin_h, (1, 2))


def _block_ones():
    idx = np.arange(LANES) // HEAD_DIM
    return jnp.asarray((idx[:, None] == idx[None, :]).astype(np.float32), dtype=BF16)


def kernel(x_prompt, x_sample, c_prompt, c_sample, w_ada, b_ada, norm_attn, norm_ffn, w_in,
           q_norm_a, k_norm_a, lambda_q1, lambda_k1, lambda_q2, lambda_k2, subln_b, rel_bias,
           w_branch_a, w_branch_b, w_out, w_router, router_bias, w_gate, w_up, w_down,
           final_norm):
    depth = w_in.shape[0]
    w_in_b = w_in.astype(BF16)
    wa_b = w_branch_a.astype(BF16)
    wb_b = w_branch_b.astype(BF16)
    wo_b = w_out.astype(BF16)
    wg_b = w_gate.astype(BF16)
    wu_b = w_up.astype(BF16)
    wd_b = w_down.astype(BF16)
    wr_pad = jnp.pad(w_router, ((0, 0), (0, LANES - N_EXPERTS)))
    wr_t = w_router.T
    bd = _block_ones()
    fn = final_norm.reshape(1, -1)

    def run(x3, c):
        bsz, seq, _ = x3.shape
        tq = min(Q_TILE, seq)
        cos_t, sin_t = _rope_tables(seq)
        bias_t = _bias_table(rel_bias, seq, tq)
        n = bsz * seq
        x = x3.reshape(n, D_MODEL)
        tri = jnp.asarray(np.triu(np.ones((min(ROW_TILE, seq),) * 2, np.float32), 1), BF16)
        delta = None
        prev_mod = None
        for l in range(depth):
            mod = _ada(c, w_ada[l], b_ada[l]).reshape(bsz * N_MOD, 1, D_MODEL)
            outs = _inproj(
                x, delta, mod, prev_mod, norm_attn[l].reshape(1, -1), w_in_b[l],
                jnp.tile(q_norm_a[l], 2).reshape(1, -1), jnp.tile(k_norm_a[l], 2).reshape(1, -1),
                cos_t, sin_t, bd, seq)
            qa, ka, va, qb, kb, vb, g = outs[:7]
            if delta is not None:
                x = outs[7]
            oa = _gqa(qa, ka, va, bsz, seq)
            lam_init = 0.8 - 0.6 * math.exp(-0.3 * l)
            ob = _diff(qb, kb, vb, bias_t, lambda_q1[l].reshape(1, -1), lambda_k1[l].reshape(1, -1),
                       lambda_q2[l].reshape(1, -1), lambda_k2[l].reshape(1, -1),
                       subln_b[l].reshape(1, -1), lam_init, bsz, seq)
            x, hx, cls, rank, counts = _merge(oa, ob, g, x, mod, norm_ffn[l].reshape(1, -1),
                                              wa_b[l], wb_b[l], wo_b[l], wr_pad, tri, router_bias,
                                              seq)
            cls = cls.reshape(n)
            rank = rank.reshape(n)
            starts, tile_lo, tile_hi, tile_valid, n_tiles = _tile_plan(
                counts[:N_CLASSES, 0].astype(jnp.int32), n)
            n_slots = n_tiles * MOE_TILE
            slot = starts[cls] + rank
            hs = _dispatch(hx.reshape(n, ROW_CHUNKS, LANES), slot, n_slots)
            ys = _moe(hs.reshape(n_slots * ROW_CHUNKS, LANES), tile_lo, tile_hi, tile_valid, wr_t,
                      wg_b[l], wu_b[l], wd_b[l])
            delta = _combine(ys.reshape(n_slots, ROW_CHUNKS, LANES), slot, n)
            delta = delta.reshape(n * ROW_CHUNKS, LANES)
            prev_mod = mod
        return _final(x, delta, prev_mod, fn, seq).reshape(bsz, seq, D_MODEL)

    return (run(x_prompt, c_prompt), run(x_sample, c_sample))
```

```python
import functools
import math

import numpy as np
import jax
import jax.numpy as jnp
from jax import lax
from jax.experimental import pallas as pl
from jax.experimental.pallas import tpu as pltpu

F32 = jnp.float32
BF16 = jnp.bfloat16

D_MODEL = 1024
GRID_W = 64
HEAD_DIM = 64
EPS = 1e-6
A_HEADS = 8
A_KV_HEADS = 2
A_WIDTH = A_HEADS * HEAD_DIM
A_KV_WIDTH = A_KV_HEADS * HEAD_DIM
ROT_FREQS = HEAD_DIM // 4
ROPE_THETA = 10000.0
B_HEADS = 4
B_V_DIM = 2 * HEAD_DIM
B_QK_WIDTH = B_HEADS * 2 * HEAD_DIM
B_WIDTH = B_HEADS * B_V_DIM
GATE_WIDTH = 2 * D_MODEL
REL_BUCKETS = 32
REL_MAX_DIST = 128
N_EXPERTS = 16
N_GROUPS = 4
EXPERTS_PER_GROUP = N_EXPERTS // N_GROUPS
D_FF_EXPERT = 512
N_MOD = 6
PAIRS_PER_GROUP = EXPERTS_PER_GROUP * (EXPERTS_PER_GROUP - 1) // 2
N_CLASSES = N_GROUPS * PAIRS_PER_GROUP
CLASS_ROWS = 32
ROW_CHUNKS = D_MODEL // 128

OFF_QA = 0
OFF_KA = OFF_QA + A_WIDTH
OFF_VA = OFF_KA + A_KV_WIDTH
OFF_QB = OFF_VA + A_KV_WIDTH
OFF_KB = OFF_QB + B_QK_WIDTH
OFF_VB = OFF_KB + B_QK_WIDTH
OFF_G = OFF_VB + B_WIDTH
IN_COLS = OFF_G + GATE_WIDTH

LOG2E = math.log2(math.e)
QK_SCALE = HEAD_DIM ** -0.5 * LOG2E

LANES = 128
VMEM_LIMIT = 56 * 1024 * 1024

ROW_TILE = 512
Q_TILE = 256
GQA_Q_TILE = 512
DIFF_HEADS_PER_STEP = 2
BIAS_CHUNK = 256
MOE_TILE = 256
DMA_ROWS = 2048
DMA_UNROLL = 8


def _params(*sem):
    return pltpu.CompilerParams(dimension_semantics=sem, vmem_limit_bytes=VMEM_LIMIT)


def _rms(x, eps=EPS):
    return x * lax.rsqrt(jnp.mean(x * x, axis=-1, keepdims=True) + eps)


def _lane_ids(shape):
    return lax.broadcasted_iota(jnp.int32, shape, len(shape) - 1)


def _ada_kernel(c_ref, w_ref, b_ref, o_ref):
    c = c_ref[...]
    a = (c * jax.nn.sigmoid(c)).astype(BF16)
    o_ref[...] = jnp.dot(a, w_ref[...].astype(BF16), preferred_element_type=F32) + b_ref[...]


def _ada(c, w_ada, b_ada):
    bsz = c.shape[0]
    ncol = w_ada.shape[1] // D_MODEL
    return pl.pallas_call(
        _ada_kernel,
        grid=(ncol,),
        in_specs=[pl.BlockSpec((bsz, D_MODEL), lambda j: (0, 0)),
                  pl.BlockSpec((D_MODEL, D_MODEL), lambda j: (0, j)),
                  pl.BlockSpec((1, D_MODEL), lambda j: (0, j))],
        out_specs=pl.BlockSpec((bsz, D_MODEL), lambda j: (0, j)),
        out_shape=jax.ShapeDtypeStruct((bsz, w_ada.shape[1]), F32),
        compiler_params=_params("arbitrary"),
        name="ada_mod",
    )(c, w_ada, b_ada.reshape(1, -1))


def _bias_kernel(rb_ref, bucket_ref, o_ref):
    bucket = bucket_ref[...]
    accs = [jnp.zeros(bucket.shape, F32) for _ in range(B_HEADS)]
    for b in range(REL_BUCKETS):
        hit = bucket == b
        for h in range(B_HEADS):
            accs[h] = jnp.where(hit, rb_ref[b * B_HEADS + h], accs[h])
    for h in range(B_HEADS):
        o_ref[h] = accs[h] * LOG2E


def _rel_bucket(rel):
    nb = REL_BUCKETS // 2
    max_exact = nb // 2
    ret = jnp.where(rel > 0, nb, 0)
    n = jnp.abs(rel)
    large = max_exact + (jnp.log(jnp.maximum(n, 1).astype(F32) / max_exact)
                         / math.log(REL_MAX_DIST / max_exact) * (nb - max_exact)).astype(jnp.int32)
    large = jnp.minimum(large, nb - 1)
    return ret + jnp.where(n < max_exact, n, large)


def _bias_table(rel_bias, seq, tq):
    nq = seq // tq
    width = seq + (nq - 1) * tq
    r = jnp.arange(tq, dtype=jnp.int32)[:, None]
    c = jnp.arange(width, dtype=jnp.int32)[None, :]
    bucket = _rel_bucket(c - (nq - 1) * tq - r).astype(jnp.int32)
    return pl.pallas_call(
        _bias_kernel,
        grid=(width // BIAS_CHUNK,),
        in_specs=[pl.BlockSpec(memory_space=pltpu.SMEM),
                  pl.BlockSpec((tq, BIAS_CHUNK), lambda j: (0, j))],
        out_specs=pl.BlockSpec((B_HEADS, tq, BIAS_CHUNK), lambda j: (0, 0, j)),
        out_shape=jax.ShapeDtypeStruct((B_HEADS, tq, width), F32),
        compiler_params=_params("arbitrary"),
        name="rel_bias_table",
    )(rel_bias.reshape(-1), bucket)


def _split_dot_ones(sq, bd):
    hi = sq.astype(BF16)
    lo = (sq - hi.astype(F32)).astype(BF16)
    return (jnp.dot(hi, bd, preferred_element_type=F32)
            + jnp.dot(lo, bd, preferred_element_type=F32))


def _head_norm_rope(y, gain, cos, sin, bd):
    ss = _split_dot_ones(y * y, bd)
    yn = y * lax.rsqrt(ss * (1.0 / HEAD_DIM) + EPS) * gain
    lane = _lane_ids(yn.shape)
    partner = jnp.where((lane & ROT_FREQS) == 0,
                        pltpu.roll(yn, LANES - ROT_FREQS, 1),
                        pltpu.roll(yn, ROT_FREQS, 1))
    return yn * cos + partner * sin


def _inproj_kernel(has_delta, *refs):
    if has_delta:
        x_ref, y_ref, gate2_ref = refs[:3]
        refs = refs[3:]
        x = x_ref[...] + gate2_ref[0] * _load_token_major(y_ref)
        xo_ref = refs[-1]
        xo_ref[...] = x
        refs = refs[:-1]
    else:
        x_ref = refs[0]
        refs = refs[1:]
        x = x_ref[...]
    (shift_ref, scale_ref, na_ref, w_ref, qn_ref, kn_ref, cos_ref, sin_ref, bd_ref,
     qa_ref, ka_ref, va_ref, qb_ref, kb_ref, vb_ref, g_ref) = refs
    h = _rms(x) * na_ref[...] * (1.0 + scale_ref[0]) + shift_ref[0]
    hb = h.astype(BF16)

    def seg(lo, width):
        return jnp.dot(hb, w_ref[:, lo:lo + width], preferred_element_type=F32)

    cos = cos_ref[...]
    sin = sin_ref[...]
    bd = bd_ref[...]

    qa = seg(OFF_QA, A_WIDTH)
    for j in range(A_WIDTH // LANES):
        y = _head_norm_rope(qa[:, j * LANES:(j + 1) * LANES], qn_ref[...], cos, sin, bd)
        qa_ref[:, j * LANES:(j + 1) * LANES] = (y * QK_SCALE).astype(BF16)

    lane = _lane_ids((x.shape[0], LANES))
    low = lane < HEAD_DIM
    kva = seg(OFF_KA, 2 * A_KV_WIDTH)
    ka = _head_norm_rope(kva[:, 0:LANES], kn_ref[...], cos, sin, bd)
    ka_sw = pltpu.roll(ka, HEAD_DIM, 1)
    ka_ref[:, 0:LANES] = jnp.where(low, ka, ka_sw).astype(BF16)
    ka_ref[:, LANES:2 * LANES] = jnp.where(low, ka_sw, ka).astype(BF16)
    va = kva[:, LANES:2 * LANES]
    va_sw = pltpu.roll(va, HEAD_DIM, 1)
    va_ref[:, 0:LANES] = jnp.where(low, va, 1.0).astype(BF16)
    va_ref[:, LANES:2 * LANES] = jnp.where(low, 1.0, va_sw).astype(BF16)
    va_ref[:, 2 * LANES:3 * LANES] = jnp.where(low, va_sw, 1.0).astype(BF16)
    va_ref[:, 3 * LANES:4 * LANES] = jnp.where(low, 1.0, va).astype(BF16)

    qb_ref[...] = (seg(OFF_QB, B_QK_WIDTH) * QK_SCALE).astype(BF16)
    kb_ref[...] = seg(OFF_KB, B_QK_WIDTH).astype(BF16)
    ones = jnp.ones((x.shape[0], B_V_DIM), BF16)
    vb = seg(OFF_VB, B_WIDTH).astype(BF16)
    for hd in range(B_HEADS):
        vb_ref[:, 2 * hd * B_V_DIM:(2 * hd + 1) * B_V_DIM] = vb[:, hd * B_V_DIM:(hd + 1) * B_V_DIM]
        vb_ref[:, (2 * hd + 1) * B_V_DIM:(2 * hd + 2) * B_V_DIM] = ones
    chunk = 512
    for j in range(GATE_WIDTH // chunk):
        g_ref[:, j * chunk:(j + 1) * chunk] = jax.nn.sigmoid(seg(OFF_G + j * chunk, chunk)).astype(BF16)


def _inproj(x, delta, mod, prev_mod, norm_attn, w_in_b, qn, kn, cos_t, sin_t, bd, seq):
    n = x.shape[0]
    tm = min(ROW_TILE, seq)
    tps = seq // tm
    has_delta = delta is not None

    def row(width):
        return pl.BlockSpec((tm, width), lambda i: (i, 0))

    def full(a):
        return pl.BlockSpec(a.shape, lambda i: (0,) * a.ndim)

    def modspec(k):
        return pl.BlockSpec((1, 1, D_MODEL), lambda i: (N_MOD * (i // tps) + k, 0, 0))

    outs = [(A_WIDTH, BF16), (2 * LANES, BF16), (4 * LANES, BF16), (B_QK_WIDTH, BF16),
            (B_QK_WIDTH, BF16), (2 * B_WIDTH, BF16), (GATE_WIDTH, BF16)]
    args = [x]
    in_specs = [row(D_MODEL)]
    if has_delta:
        args += [delta, prev_mod]
        in_specs += [pl.BlockSpec((tm * ROW_CHUNKS, LANES), lambda i: (i, 0)), modspec(5)]
        outs = outs + [(D_MODEL, F32)]
    args += [mod, mod, norm_attn, w_in_b, qn, kn, cos_t, sin_t, bd]
    in_specs += [modspec(0), modspec(1), full(norm_attn), full(w_in_b), full(qn), full(kn),
                 pl.BlockSpec((tm, LANES), lambda i: (i % tps, 0)),
                 pl.BlockSpec((tm, LANES), lambda i: (i % tps, 0)),
                 full(bd)]
    return pl.pallas_call(
        functools.partial(_inproj_kernel, has_delta),
        grid=(n // tm,),
        in_specs=in_specs,
        out_specs=[row(w) for w, _ in outs],
        out_shape=[jax.ShapeDtypeStruct((n, w), d) for w, d in outs],
        compiler_params=_params("arbitrary"),
        name="in_proj",
    )(*args)


_NT = (((1,), (1,)), ((), ()))


def _exp_scores(s):
    return jnp.exp2(s - jnp.max(s, axis=-1, keepdims=True)).astype(BF16)


def _gqa_kernel(q_ref, k_ref, v_ref, o_ref):
    k = k_ref[...]
    low = _lane_ids((q_ref.shape[0], LANES)) < HEAD_DIM
    for pair in range(q_ref.shape[1] // LANES):
        q = q_ref[:, pair * LANES:(pair + 1) * LANES]
        zero = jnp.zeros_like(q)
        normed = []
        for half, qh in enumerate((jnp.where(low, q, zero), jnp.where(low, zero, q))):
            e = _exp_scores(lax.dot_general(qh, k, _NT, preferred_element_type=F32))
            ol = jnp.dot(e, v_ref[:, half * LANES:(half + 1) * LANES], preferred_element_type=F32)
            normed.append(ol / pltpu.roll(ol, HEAD_DIM, 1))
        o_ref[:, pair * LANES:(pair + 1) * LANES] = jnp.where(low, normed[0], normed[1]).astype(
            o_ref.dtype)


def _gqa(qa, ka, va, bsz, seq):
    n = qa.shape[0]
    tq = min(GQA_Q_TILE, seq)
    nq = seq // tq
    group_width = A_WIDTH // A_KV_HEADS
    return pl.pallas_call(
        _gqa_kernel,
        grid=(bsz, A_KV_HEADS, nq),
        in_specs=[pl.BlockSpec((tq, group_width), lambda b, j, t: (b * nq + t, j)),
                  pl.BlockSpec((seq, LANES), lambda b, j, t: (b, j)),
                  pl.BlockSpec((seq, 2 * LANES), lambda b, j, t: (b, j))],
        out_specs=pl.BlockSpec((tq, group_width), lambda b, j, t: (b * nq + t, j)),
        out_shape=jax.ShapeDtypeStruct((n, A_WIDTH), BF16),
        compiler_params=_params("arbitrary", "arbitrary", "arbitrary"),
        name="gqa_attention",
    )(qa, ka, va)


def _diff_kernel(lam_init, nq, seq, q_ref, k_ref, v_ref, bias_ref, lq1_ref, lk1_ref, lq2_ref,
                 lk2_ref, sub_ref, o_ref):
    tq = q_ref.shape[0]
    t = pl.program_id(2)
    lam = (jnp.exp(jnp.sum(lq1_ref[...] * lk1_ref[...], axis=-1, keepdims=True))
           - jnp.exp(jnp.sum(lq2_ref[...] * lk2_ref[...], axis=-1, keepdims=True)) + lam_init)
    low = _lane_ids((tq, LANES)) < HEAD_DIM
    start = pl.multiple_of((nq - 1 - t) * tq, LANES)
    for hd in range(DIFF_HEADS_PER_STEP):
        q = q_ref[:, hd * LANES:(hd + 1) * LANES]
        k = k_ref[:, hd * LANES:(hd + 1) * LANES]
        v = v_ref[:, 2 * hd * B_V_DIM:(2 * hd + 2) * B_V_DIM]
        zero = jnp.zeros_like(q)
        bias = bias_ref[hd, :, pl.ds(start, seq)]
        parts = []
        for qh in (jnp.where(low, q, zero), jnp.where(low, zero, q)):
            e = _exp_scores(lax.dot_general(qh, k, _NT, preferred_element_type=F32) + bias)
            ol = jnp.dot(e, v, preferred_element_type=F32)
            parts.append(ol[:, 0:B_V_DIM] / ol[:, B_V_DIM:2 * B_V_DIM])
        o = parts[0] - lam * parts[1]
        o_ref[:, hd * B_V_DIM:(hd + 1) * B_V_DIM] = (
            _rms(o) * sub_ref[...] * (1.0 - lam_init)).astype(o_ref.dtype)


def _diff(qb, kb, vb, bias_t, lq1, lk1, lq2, lk2, subln, lam_init, bsz, seq):
    n = qb.shape[0]
    tq = bias_t.shape[1]
    nq = seq // tq
    hps = DIFF_HEADS_PER_STEP

    def small(a):
        return pl.BlockSpec(a.shape, lambda h, b, t: (0, 0))

    return pl.pallas_call(
        functools.partial(_diff_kernel, lam_init, nq, seq),
        grid=(B_HEADS // hps, bsz, nq),
        in_specs=[pl.BlockSpec((tq, hps * LANES), lambda h, b, t: (b * nq + t, h)),
                  pl.BlockSpec((seq, hps * LANES), lambda h, b, t: (b, h)),
                  pl.BlockSpec((seq, hps * 2 * B_V_DIM), lambda h, b, t: (b, h)),
                  pl.BlockSpec((hps, tq, bias_t.shape[2]), lambda h, b, t: (h, 0, 0),
                               pipeline_mode=pl.Buffered(1)),
                  small(lq1), small(lk1), small(lq2), small(lk2), small(subln)],
        out_specs=pl.BlockSpec((tq, hps * B_V_DIM), lambda h, b, t: (b * nq + t, h)),
        out_shape=jax.ShapeDtypeStruct((n, B_WIDTH), BF16),
        compiler_params=_params("arbitrary", "arbitrary", "arbitrary"),
        name="diff_attention",
    )(qb, kb, vb, bias_t, lq1, lk1, lq2, lk2, subln)


def _load_token_major(ref):
    rows = ref.shape[0] // ROW_CHUNKS
    return jnp.concatenate([ref[pl.ds(c, rows, stride=ROW_CHUNKS), :] for c in range(ROW_CHUNKS)],
                           axis=-1)


def _store_token_major(ref, val):
    rows = val.shape[0]
    for c in range(ROW_CHUNKS):
        ref[pl.ds(c, rows, stride=ROW_CHUNKS), :] = val[:, c * LANES:(c + 1) * LANES]


def _route(biased):
    epg = EXPERTS_PER_GROUP
    group_score = []
    for g in range(N_GROUPS):
        v = biased[g * epg:(g + 1) * epg]
        best = None
        for a in range(epg):
            for b in range(a + 1, epg):
                pair = v[a] + v[b]
                best = pair if best is None else jnp.maximum(best, pair)
        group_score.append(best)
    sel = jnp.zeros_like(group_score[0], dtype=jnp.int32)
    top = group_score[0]
    for g in range(1, N_GROUPS):
        better = group_score[g] > top
        sel = jnp.where(better, g, sel)
        top = jnp.where(better, group_score[g], top)

    def pick(vals, k):
        out = vals[k]
        for g in range(1, N_GROUPS):
            out = jnp.where(sel == g, vals[g * epg + k], out)
        return out

    bsel = [pick(biased, k) for k in range(epg)]
    neg = jnp.full_like(bsel[0], -jnp.inf)

    def argmax_first(vals):
        idx = jnp.zeros_like(sel)
        top_v = vals[0]
        for k in range(1, epg):
            better = vals[k] > top_v
            idx = jnp.where(better, k, idx)
            top_v = jnp.where(better, vals[k], top_v)
        return idx

    i1 = argmax_first(bsel)
    i2 = argmax_first([jnp.where(i1 == k, neg, bsel[k]) for k in range(epg)])

    k_lo = jnp.minimum(i1, i2)
    k_hi = jnp.maximum(i1, i2)
    pair = jnp.where(k_lo == 0, k_hi - 1, jnp.where(k_lo == 1, k_hi + 1, PAIRS_PER_GROUP - 1))
    return sel * PAIRS_PER_GROUP + pair


def _merge_kernel(oa_ref, ob_ref, g_ref, x_ref, gate1_ref, shift_ref, scale_ref, nf_ref, wa_ref,
                  wb_ref, wo_ref, wr_ref, tri_ref, rb_ref, x2_ref, hx_ref, cls_ref, rank_ref,
                  count_ref, carry_ref):
    @pl.when(pl.program_id(0) == 0)
    def _():
        carry_ref[...] = jnp.zeros_like(carry_ref)

    ya = jnp.dot(oa_ref[...], wa_ref[...], preferred_element_type=F32)
    yb = jnp.dot(ob_ref[...], wb_ref[...], preferred_element_type=F32)
    m = g_ref[:, 0:D_MODEL].astype(F32) * ya + g_ref[:, D_MODEL:2 * D_MODEL].astype(F32) * yb
    merged = jnp.dot(m.astype(BF16), wo_ref[...], preferred_element_type=F32)
    x2 = x_ref[...] + gate1_ref[0] * merged
    x2_ref[...] = x2
    h2 = _rms(x2) * nf_ref[...] * (1.0 + scale_ref[0]) + shift_ref[0]
    _store_token_major(hx_ref, h2)

    h_hi = h2.astype(BF16)
    h_lo = (h2 - h_hi.astype(F32)).astype(BF16)
    by_hi = jnp.dot(h_hi, wr_ref[...], preferred_element_type=F32)
    logits = (by_hi[:, 0:LANES] + by_hi[:, LANES:2 * LANES]
              + jnp.dot(h_lo, wr_ref[:, 0:LANES], preferred_element_type=F32))
    scores_t = jnp.transpose(jax.nn.sigmoid(logits))
    biased = [scores_t[e:e + 1, :] + rb_ref[e] for e in range(N_EXPERTS)]
    cls = _route(biased)
    rows = scores_t.shape[1]

    class_row = lax.broadcasted_iota(jnp.int32, (CLASS_ROWS, rows), 0)
    onehot = class_row == cls
    before = jnp.dot(jnp.where(onehot, 1.0, 0.0).astype(BF16), tri_ref[...],
                     preferred_element_type=F32)
    carry = carry_ref[...]
    rank = jnp.sum(jnp.where(onehot, before + carry[:, 0:1], 0.0), axis=0, keepdims=True)
    carry = carry + jnp.sum(jnp.where(onehot, 1.0, 0.0), axis=1, keepdims=True)
    carry_ref[...] = carry
    count_ref[...] = carry
    cls_ref[0] = cls
    rank_ref[0] = rank.astype(jnp.int32)


def _merge(oa, ob, g, x, mod, norm_ffn, wa_b, wb_b, wo_b, wr_pad, tri, router_bias, seq):
    n = x.shape[0]
    tm = min(ROW_TILE, seq)
    tps = seq // tm
    nt = n // tm

    def row(width):
        return pl.BlockSpec((tm, width), lambda i: (i, 0))

    def full(a):
        return pl.BlockSpec(a.shape, lambda i: (0,) * a.ndim)

    def modspec(k):
        return pl.BlockSpec((1, 1, D_MODEL), lambda i: (N_MOD * (i // tps) + k, 0, 0))

    lane_vec = pl.BlockSpec((1, 1, tm), lambda i: (i, 0, 0))
    return pl.pallas_call(
        _merge_kernel,
        grid=(nt,),
        in_specs=[row(A_WIDTH), row(B_WIDTH), row(GATE_WIDTH), row(D_MODEL), modspec(2),
                  modspec(3), modspec(4), full(norm_ffn), full(wa_b), full(wb_b), full(wo_b),
                  full(wr_pad), full(tri), pl.BlockSpec(memory_space=pltpu.SMEM)],
        out_specs=[row(D_MODEL), pl.BlockSpec((tm * ROW_CHUNKS, LANES), lambda i: (i, 0)),
                   lane_vec, lane_vec, pl.BlockSpec((CLASS_ROWS, LANES), lambda i: (0, 0))],
        out_shape=[jax.ShapeDtypeStruct((n, D_MODEL), F32),
                   jax.ShapeDtypeStruct((n * ROW_CHUNKS, LANES), F32),
                   jax.ShapeDtypeStruct((nt, 1, tm), jnp.int32),
                   jax.ShapeDtypeStruct((nt, 1, tm), jnp.int32),
                   jax.ShapeDtypeStruct((CLASS_ROWS, LANES), F32)],
        scratch_shapes=[pltpu.VMEM((CLASS_ROWS, LANES), F32)],
        compiler_params=_params("arbitrary"),
        name="merge_route",
    )(oa, ob, g, x, mod, mod, mod, norm_ffn, wa_b, wb_b, wo_b, wr_pad, tri, router_bias)


def _row_copy(src_ref, src_row, dst_ref, dst_row, sem):
    return pltpu.make_async_copy(src_ref.at[src_row], dst_ref.at[dst_row], sem)


def _permute_rows(to_slot, slot_ref, tok_ref, slot_major_ref, sem):
    rows = tok_ref.shape[0]

    def body(j, carry):
        first = j * DMA_UNROLL
        slots = [slot_ref[first + k] for k in range(DMA_UNROLL)]
        for k in range(DMA_UNROLL):
            if to_slot:
                _row_copy(tok_ref, first + k, slot_major_ref, slots[k], sem).start()
            else:
                _row_copy(slot_major_ref, slots[k], tok_ref, first + k, sem).start()
        return carry

    lax.fori_loop(0, rows // DMA_UNROLL, body, 0)
    pltpu.make_async_copy(slot_major_ref.at[pl.ds(0, rows)], tok_ref, sem).wait()


def _dispatch_kernel(slot_ref, src_ref, init_ref, dst_ref, sem):
    del init_ref
    _permute_rows(True, slot_ref, src_ref, dst_ref, sem)


def _dispatch(hx, slot, n_slots):
    n = hx.shape[0]
    rows = min(DMA_ROWS, n)
    hbm = pl.BlockSpec(memory_space=pl.ANY)
    return pl.pallas_call(
        _dispatch_kernel,
        grid=(n // rows,),
        in_specs=[pl.BlockSpec((rows,), lambda i: (i,), memory_space=pltpu.SMEM),
                  pl.BlockSpec((rows,) + hx.shape[1:], lambda i: (i, 0, 0)), hbm],
        out_specs=hbm,
        out_shape=jax.ShapeDtypeStruct((n_slots,) + hx.shape[1:], hx.dtype),
        scratch_shapes=[pltpu.SemaphoreType.DMA(())],
        input_output_aliases={2: 0},
        compiler_params=_params("arbitrary"),
        name="dispatch_rows",
    )(slot, hx, jnp.zeros((n_slots,) + hx.shape[1:], hx.dtype))


def _combine_kernel(slot_ref, src_ref, dst_ref, sem):
    _permute_rows(False, slot_ref, dst_ref, src_ref, sem)


def _combine(ys, slot, n):
    rows = min(DMA_ROWS, n)
    return pl.pallas_call(
        _combine_kernel,
        grid=(n // rows,),
        in_specs=[pl.BlockSpec((rows,), lambda i: (i,), memory_space=pltpu.SMEM),
                  pl.BlockSpec(memory_space=pl.ANY)],
        out_specs=pl.BlockSpec((rows,) + ys.shape[1:], lambda i: (i, 0, 0)),
        out_shape=jax.ShapeDtypeStruct((n,) + ys.shape[1:], ys.dtype),
        scratch_shapes=[pltpu.SemaphoreType.DMA(())],
        compiler_params=_params("arbitrary"),
        name="combine_rows",
    )(slot, ys)


def _moe_kernel(lo_ref, hi_ref, valid_ref, hs_ref, wr_ref, wg_lo, wu_lo, wd_lo, wg_hi, wu_hi,
                wd_hi, o_ref):
    t = pl.program_id(0)
    valid = valid_ref[t] > 0

    @pl.when(jnp.logical_not(valid))
    def _():
        o_ref[...] = jnp.zeros_like(o_ref)

    @pl.when(valid)
    def _():
        h32 = _load_token_major(hs_ref)
        h = h32.astype(BF16)

        def ffn(wg_ref, wu_ref, wd_ref):
            a = jnp.dot(h, wg_ref[0], preferred_element_type=F32)
            u = jnp.dot(h, wu_ref[0], preferred_element_type=F32)
            act = (a * jax.nn.sigmoid(a) * u).astype(BF16)
            return jnp.dot(act, wd_ref[0], preferred_element_type=F32)

        def score(e):
            return jax.nn.sigmoid(jnp.sum(h32 * wr_ref[pl.ds(e, 1), :], axis=-1, keepdims=True))

        s_lo = score(lo_ref[t])
        s_hi = score(hi_ref[t])
        denom = s_lo + s_hi
        y = ((s_lo / denom) * ffn(wg_lo, wu_lo, wd_lo)
             + (s_hi / denom) * ffn(wg_hi, wu_hi, wd_hi))
        _store_token_major(o_ref, y)


def _moe(hs, tile_lo, tile_hi, tile_valid, wr_t, wg_b, wu_b, wd_b):
    n_tiles = tile_lo.shape[0]
    up = (1, D_MODEL, D_FF_EXPERT)
    down = (1, D_FF_EXPERT, D_MODEL)

    def by(sel):
        return lambda t, lo, hi, valid: ((lo, hi)[sel][t], 0, 0)

    rows = pl.BlockSpec((MOE_TILE * ROW_CHUNKS, LANES), lambda t, lo, hi, valid: (t, 0))
    return pl.pallas_call(
        _moe_kernel,
        grid_spec=pltpu.PrefetchScalarGridSpec(
            num_scalar_prefetch=3,
            grid=(n_tiles,),
            in_specs=[rows, pl.BlockSpec(wr_t.shape, lambda t, lo, hi, valid: (0, 0)),
                      pl.BlockSpec(up, by(0)), pl.BlockSpec(up, by(0)), pl.BlockSpec(down, by(0)),
                      pl.BlockSpec(up, by(1)), pl.BlockSpec(up, by(1)), pl.BlockSpec(down, by(1))],
            out_specs=rows),
        out_shape=jax.ShapeDtypeStruct(hs.shape, F32),
        compiler_params=_params("arbitrary"),
        name="experts",
    )(tile_lo, tile_hi, tile_valid, hs, wr_t, wg_b, wu_b, wd_b, wg_b, wu_b, wd_b)


def _tile_plan(counts, n):
    n_tiles = n // MOE_TILE + N_CLASSES
    tiles_c = (counts + MOE_TILE - 1) // MOE_TILE
    ends = jnp.cumsum(tiles_c)
    starts = (ends - tiles_c) * MOE_TILE
    total = ends[-1]
    t = jnp.arange(n_tiles, dtype=jnp.int32)
    last = jnp.maximum(total - 1, 0)
    tile_cls = jnp.sum((jnp.minimum(t, last)[:, None] >= ends[None, :]).astype(jnp.int32), axis=1)
    tile_cls = jnp.minimum(tile_cls, N_CLASSES - 1)
    k_lo, k_hi = np.triu_indices(EXPERTS_PER_GROUP, 1)
    group = np.arange(N_CLASSES) // PAIRS_PER_GROUP
    lo_table = jnp.asarray(group * EXPERTS_PER_GROUP + np.tile(k_lo, N_GROUPS), jnp.int32)
    hi_table = jnp.asarray(group * EXPERTS_PER_GROUP + np.tile(k_hi, N_GROUPS), jnp.int32)
    return (starts.astype(jnp.int32), lo_table[tile_cls], hi_table[tile_cls],
            (t < total).astype(jnp.int32), n_tiles)


def _final_kernel(x_ref, y_ref, gate2_ref, fn_ref, o_ref):
    o_ref[...] = _rms(x_ref[...] + gate2_ref[0] * _load_token_major(y_ref)) * fn_ref[...]


def _final(x2, y, mod, final_norm, seq):
    n = x2.shape[0]
    tm = min(ROW_TILE, seq)
    tps = seq // tm
    row = pl.BlockSpec((tm, D_MODEL), lambda i: (i, 0))
    return pl.pallas_call(
        _final_kernel,
        grid=(n // tm,),
        in_specs=[row, pl.BlockSpec((tm * ROW_CHUNKS, LANES), lambda i: (i, 0)),
                  pl.BlockSpec((1, 1, D_MODEL), lambda i: (N_MOD * (i // tps) + 5, 0, 0)),
                  pl.BlockSpec(final_norm.shape, lambda i: (0, 0))],
        out_specs=row,
        out_shape=jax.ShapeDtypeStruct((n, D_MODEL), F32),
        compiler_params=_params("arbitrary"),
        name="final_norm",
    )(x2, y, mod, final_norm)


def _rope_tables(seq):
    rows = seq // GRID_W
    row = jnp.repeat(jnp.arange(rows, dtype=F32), GRID_W)
    col = jnp.tile(jnp.arange(GRID_W, dtype=F32), rows)
    inv = ROPE_THETA ** (-jnp.arange(ROT_FREQS, dtype=F32) / ROT_FREQS)
    ang = jnp.stack([row[:, None] * inv, col[:, None] * inv], axis=1)
    cos = jnp.cos(ang)
    sin = jnp.sin(ang)
    cos_h = jnp.concatenate([cos, cos], axis=-1).reshape(seq, HEAD_DIM)
    sin_h = jnp.concatenate([-sin, sin], axis=-1).reshape(seq, HEAD_DIM)
    return jnp.tile(cos_h, (1, 2)), jnp.tile(sin_h, (1, 2))


def _block_ones():
    idx = np.arange(LANES) // HEAD_DIM
    return jnp.asarray((idx[:, None] == idx[None, :]).astype(np.float32), dtype=BF16)


def kernel(x_prompt, x_sample, c_prompt, c_sample, w_ada, b_ada, norm_attn, norm_ffn, w_in,
           q_norm_a, k_norm_a, lambda_q1, lambda_k1, lambda_q2, lambda_k2, subln_b, rel_bias,
           w_branch_a, w_branch_b, w_out, w_router, router_bias, w_gate, w_up, w_down,
           final_norm):
    depth = w_in.shape[0]
    w_in_b = w_in.astype(BF16)
    wa_b = w_branch_a.astype(BF16)
    wb_b = w_branch_b.astype(BF16)
    wo_b = w_out.astype(BF16)
    wg_b = w_gate.astype(BF16)
    wu_b = w_up.astype(BF16)
    wd_b = w_down.astype(BF16)
    wr_hi = w_router.astype(BF16)
    wr_lo = (w_router - wr_hi.astype(F32)).astype(BF16)
    lane_pad = ((0, 0), (0, LANES - N_EXPERTS))
    wr_pad = jnp.concatenate([jnp.pad(wr_hi, lane_pad), jnp.pad(wr_lo, lane_pad)], axis=1)
    wr_t = w_router.T
    bd = _block_ones()
    fn = final_norm.reshape(1, -1)

    def run(x3, c):
        bsz, seq, _ = x3.shape
        tq = min(Q_TILE, seq)
        cos_t, sin_t = _rope_tables(seq)
        bias_t = _bias_table(rel_bias, seq, tq)
        n = bsz * seq
        x = x3.reshape(n, D_MODEL)
        tri = jnp.asarray(np.triu(np.ones((min(ROW_TILE, seq),) * 2, np.float32), 1), BF16)
        delta = None
        prev_mod = None
        for l in range(depth):
            mod = _ada(c, w_ada[l], b_ada[l]).reshape(bsz * N_MOD, 1, D_MODEL)
            outs = _inproj(
                x, delta, mod, prev_mod, norm_attn[l].reshape(1, -1), w_in_b[l],
                jnp.tile(q_norm_a[l], 2).reshape(1, -1), jnp.tile(k_norm_a[l], 2).reshape(1, -1),
                cos_t, sin_t, bd, seq)
            qa, ka, va, qb, kb, vb, g = outs[:7]
            if delta is not None:
                x = outs[7]
            oa = _gqa(qa, ka, va, bsz, seq)
            lam_init = 0.8 - 0.6 * math.exp(-0.3 * l)
            ob = _diff(qb, kb, vb, bias_t, lambda_q1[l].reshape(1, -1), lambda_k1[l].reshape(1, -1),
                       lambda_q2[l].reshape(1, -1), lambda_k2[l].reshape(1, -1),
                       subln_b[l].reshape(1, -1), lam_init, bsz, seq)
            x, hx, cls, rank, counts = _merge(oa, ob, g, x, mod, norm_ffn[l].reshape(1, -1),
                                              wa_b[l], wb_b[l], wo_b[l], wr_pad, tri, router_bias,
                                              seq)
            cls = cls.reshape(n)
            rank = rank.reshape(n)
            starts, tile_lo, tile_hi, tile_valid, n_tiles = _tile_plan(
                counts[:N_CLASSES, 0].astype(jnp.int32), n)
            n_slots = n_tiles * MOE_TILE
            slot = starts[cls] + rank
            hs = _dispatch(hx.reshape(n, ROW_CHUNKS, LANES), slot, n_slots)
            ys = _moe(hs.reshape(n_slots * ROW_CHUNKS, LANES), tile_lo, tile_hi, tile_valid, wr_t,
                      wg_b[l], wu_b[l], wd_b[l])
            delta = _combine(ys.reshape(n_slots, ROW_CHUNKS, LANES), slot, n)
            delta = delta.reshape(n * ROW_CHUNKS, LANES)
            prev_mod = mod
        return _final(x, delta, prev_mod, fn, seq).reshape(bsz, seq, D_MODEL)

    return (run(x_prompt, c_prompt), run(x_sample, c_sample))
```

```python
import functools
import math

import numpy as np
import jax
import jax.numpy as jnp
from jax import lax
from jax.experimental import pallas as pl
from jax.experimental.pallas import tpu as pltpu

F32 = jnp.float32
BF16 = jnp.bfloat16

D_MODEL = 1024
GRID_W = 64
HEAD_DIM = 64
EPS = 1e-6
A_HEADS = 8
A_KV_HEADS = 2
A_WIDTH = A_HEADS * HEAD_DIM
A_KV_WIDTH = A_KV_HEADS * HEAD_DIM
ROT_FREQS = HEAD_DIM // 4
ROPE_THETA = 10000.0
B_HEADS = 4
B_V_DIM = 2 * HEAD_DIM
B_QK_WIDTH = B_HEADS * 2 * HEAD_DIM
B_WIDTH = B_HEADS * B_V_DIM
GATE_WIDTH = 2 * D_MODEL
REL_BUCKETS = 32
REL_MAX_DIST = 128
N_EXPERTS = 16
N_GROUPS = 4
EXPERTS_PER_GROUP = N_EXPERTS // N_GROUPS
D_FF_EXPERT = 512
N_MOD = 6
PAIRS_PER_GROUP = EXPERTS_PER_GROUP * (EXPERTS_PER_GROUP - 1) // 2
N_CLASSES = N_GROUPS * PAIRS_PER_GROUP
CLASS_ROWS = 32
ROW_CHUNKS = D_MODEL // 128

OFF_QA = 0
OFF_KA = OFF_QA + A_WIDTH
OFF_VA = OFF_KA + A_KV_WIDTH
OFF_QB = OFF_VA + A_KV_WIDTH
OFF_KB = OFF_QB + B_QK_WIDTH
OFF_VB = OFF_KB + B_QK_WIDTH
OFF_G = OFF_VB + B_WIDTH
IN_COLS = OFF_G + GATE_WIDTH

LOG2E = math.log2(math.e)
QK_SCALE = HEAD_DIM ** -0.5 * LOG2E

LANES = 128
VMEM_LIMIT = 56 * 1024 * 1024

ROW_TILE = 512
Q_TILE = 256
GQA_Q_TILE = 512
GQA_KV_PER_STEP = 1
DIFF_HEADS_PER_STEP = 4
BIAS_CHUNK = 256
MOE_TILE = 256
DMA_ROWS = 2048
DMA_UNROLL = 8


def _params(*sem):
    return pltpu.CompilerParams(dimension_semantics=sem, vmem_limit_bytes=VMEM_LIMIT)


def _rms(x, eps=EPS):
    return x * lax.rsqrt(jnp.mean(x * x, axis=-1, keepdims=True) + eps)


def _lane_ids(shape):
    return lax.broadcasted_iota(jnp.int32, shape, len(shape) - 1)


def _ada_kernel(c_ref, w_ref, b_ref, o_ref):
    c = c_ref[...]
    a = (c * jax.nn.sigmoid(c)).astype(BF16)
    o_ref[...] = jnp.dot(a, w_ref[...].astype(BF16), preferred_element_type=F32) + b_ref[...]


def _ada(c, w_ada, b_ada):
    bsz = c.shape[0]
    ncol = w_ada.shape[1] // D_MODEL
    return pl.pallas_call(
        _ada_kernel,
        grid=(ncol,),
        in_specs=[pl.BlockSpec((bsz, D_MODEL), lambda j: (0, 0)),
                  pl.BlockSpec((D_MODEL, D_MODEL), lambda j: (0, j)),
                  pl.BlockSpec((1, D_MODEL), lambda j: (0, j))],
        out_specs=pl.BlockSpec((bsz, D_MODEL), lambda j: (0, j)),
        out_shape=jax.ShapeDtypeStruct((bsz, w_ada.shape[1]), F32),
        compiler_params=_params("arbitrary"),
        name="ada_mod",
    )(c, w_ada, b_ada.reshape(1, -1))


def _bias_kernel(rb_ref, bucket_ref, o_ref):
    bucket = bucket_ref[...]
    accs = [jnp.zeros(bucket.shape, F32) for _ in range(B_HEADS)]
    for b in range(REL_BUCKETS):
        hit = bucket == b
        for h in range(B_HEADS):
            accs[h] = jnp.where(hit, rb_ref[b * B_HEADS + h], accs[h])
    for h in range(B_HEADS):
        o_ref[h] = accs[h] * LOG2E


def _rel_bucket(rel):
    nb = REL_BUCKETS // 2
    max_exact = nb // 2
    ret = jnp.where(rel > 0, nb, 0)
    n = jnp.abs(rel)
    large = max_exact + (jnp.log(jnp.maximum(n, 1).astype(F32) / max_exact)
                         / math.log(REL_MAX_DIST / max_exact) * (nb - max_exact)).astype(jnp.int32)
    large = jnp.minimum(large, nb - 1)
    return ret + jnp.where(n < max_exact, n, large)


def _bias_table(rel_bias, seq, tq):
    nq = seq // tq
    width = seq + (nq - 1) * tq
    r = jnp.arange(tq, dtype=jnp.int32)[:, None]
    c = jnp.arange(width, dtype=jnp.int32)[None, :]
    bucket = _rel_bucket(c - (nq - 1) * tq - r).astype(jnp.int32)
    return pl.pallas_call(
        _bias_kernel,
        grid=(width // BIAS_CHUNK,),
        in_specs=[pl.BlockSpec(memory_space=pltpu.SMEM),
                  pl.BlockSpec((tq, BIAS_CHUNK), lambda j: (0, j))],
        out_specs=pl.BlockSpec((B_HEADS, tq, BIAS_CHUNK), lambda j: (0, 0, j)),
        out_shape=jax.ShapeDtypeStruct((B_HEADS, tq, width), F32),
        compiler_params=_params("arbitrary"),
        name="rel_bias_table",
    )(rel_bias.reshape(-1), bucket)


def _split_dot_ones(sq, bd):
    hi = sq.astype(BF16)
    lo = (sq - hi.astype(F32)).astype(BF16)
    return (jnp.dot(hi, bd, preferred_element_type=F32)
            + jnp.dot(lo, bd, preferred_element_type=F32))


def _head_norm_rope(y, gain, cos, sin, bd):
    ss = _split_dot_ones(y * y, bd)
    yn = y * lax.rsqrt(ss * (1.0 / HEAD_DIM) + EPS) * gain
    lane = _lane_ids(yn.shape)
    partner = jnp.where((lane & ROT_FREQS) == 0,
                        pltpu.roll(yn, LANES - ROT_FREQS, 1),
                        pltpu.roll(yn, ROT_FREQS, 1))
    return yn * cos + partner * sin


def _inproj_kernel(has_delta, *refs):
    if has_delta:
        x_ref, y_ref, gate2_ref = refs[:3]
        refs = refs[3:]
        x = x_ref[...] + gate2_ref[0] * _load_token_major(y_ref)
        xo_ref = refs[-1]
        xo_ref[...] = x
        refs = refs[:-1]
    else:
        x_ref = refs[0]
        refs = refs[1:]
        x = x_ref[...]
    (shift_ref, scale_ref, na_ref, w_ref, qn_ref, kn_ref, cos_ref, sin_ref, bd_ref,
     qa_ref, ka_ref, va_ref, qb_ref, kb_ref, vb_ref, g_ref) = refs
    h = _rms(x) * na_ref[...] * (1.0 + scale_ref[0]) + shift_ref[0]
    hb = h.astype(BF16)

    def seg(lo, width):
        return jnp.dot(hb, w_ref[:, lo:lo + width], preferred_element_type=F32)

    cos = cos_ref[...]
    sin = sin_ref[...]
    bd = bd_ref[...]

    qa = seg(OFF_QA, A_WIDTH)
    for j in range(A_WIDTH // LANES):
        y = _head_norm_rope(qa[:, j * LANES:(j + 1) * LANES], qn_ref[...], cos, sin, bd)
        qa_ref[:, j * LANES:(j + 1) * LANES] = (y * QK_SCALE).astype(BF16)

    lane = _lane_ids((x.shape[0], LANES))
    low = lane < HEAD_DIM
    kva = seg(OFF_KA, 2 * A_KV_WIDTH)
    ka = _head_norm_rope(kva[:, 0:LANES], kn_ref[...], cos, sin, bd)
    ka_sw = pltpu.roll(ka, HEAD_DIM, 1)
    ka_ref[:, 0:LANES] = jnp.where(low, ka, ka_sw).astype(BF16)
    ka_ref[:, LANES:2 * LANES] = jnp.where(low, ka_sw, ka).astype(BF16)
    va = kva[:, LANES:2 * LANES]
    va_sw = pltpu.roll(va, HEAD_DIM, 1)
    va_ref[:, 0:LANES] = jnp.where(low, va, 1.0).astype(BF16)
    va_ref[:, LANES:2 * LANES] = jnp.where(low, 1.0, va_sw).astype(BF16)
    va_ref[:, 2 * LANES:3 * LANES] = jnp.where(low, va_sw, 1.0).astype(BF16)
    va_ref[:, 3 * LANES:4 * LANES] = jnp.where(low, 1.0, va).astype(BF16)

    qb_ref[...] = (seg(OFF_QB, B_QK_WIDTH) * QK_SCALE).astype(BF16)
    kb_ref[...] = seg(OFF_KB, B_QK_WIDTH).astype(BF16)
    ones = jnp.ones((x.shape[0], B_V_DIM), BF16)
    vb = seg(OFF_VB, B_WIDTH).astype(BF16)
    for hd in range(B_HEADS):
        vb_ref[:, 2 * hd * B_V_DIM:(2 * hd + 1) * B_V_DIM] = vb[:, hd * B_V_DIM:(hd + 1) * B_V_DIM]
        vb_ref[:, (2 * hd + 1) * B_V_DIM:(2 * hd + 2) * B_V_DIM] = ones
    chunk = 512
    for j in range(GATE_WIDTH // chunk):
        g_ref[:, j * chunk:(j + 1) * chunk] = jax.nn.sigmoid(seg(OFF_G + j * chunk, chunk)).astype(BF16)


def _inproj(x, delta, mod, prev_mod, norm_attn, w_in_b, qn, kn, cos_t, sin_t, bd, seq):
    n = x.shape[0]
    tm = min(ROW_TILE, seq)
    tps = seq // tm
    has_delta = delta is not None

    def row(width):
        return pl.BlockSpec((tm, width), lambda i: (i, 0))

    def full(a):
        return pl.BlockSpec(a.shape, lambda i: (0,) * a.ndim, pipeline_mode=pl.Buffered(1))

    def modspec(k):
        return pl.BlockSpec((1, 1, D_MODEL), lambda i: (N_MOD * (i // tps) + k, 0, 0))

    def row_out(width, dtype):
        return row(width), jax.ShapeDtypeStruct((n, width), dtype)

    outs = [row_out(A_WIDTH, BF16), row_out(2 * LANES, BF16), row_out(4 * LANES, BF16),
            row_out(B_QK_WIDTH, BF16), row_out(B_QK_WIDTH, BF16), row_out(2 * B_WIDTH, BF16),
            row_out(GATE_WIDTH, BF16)]
    args = [x]
    in_specs = [row(D_MODEL)]
    if has_delta:
        args += [delta, prev_mod]
        in_specs += [pl.BlockSpec((tm * ROW_CHUNKS, LANES), lambda i: (i, 0)), modspec(5)]
        outs = outs + [row_out(D_MODEL, F32)]
    args += [mod, mod, norm_attn, w_in_b, qn, kn, cos_t, sin_t, bd]
    in_specs += [modspec(0), modspec(1), full(norm_attn), full(w_in_b), full(qn), full(kn),
                 pl.BlockSpec((tm, LANES), lambda i: (i % tps, 0)),
                 pl.BlockSpec((tm, LANES), lambda i: (i % tps, 0)),
                 full(bd)]
    return pl.pallas_call(
        functools.partial(_inproj_kernel, has_delta),
        grid=(n // tm,),
        in_specs=in_specs,
        out_specs=[spec for spec, _ in outs],
        out_shape=[shape for _, shape in outs],
        compiler_params=_params("arbitrary"),
        name="in_proj",
    )(*args)


_NT = (((1,), (1,)), ((), ()))


def _exp_scores(s):
    return jnp.exp2(s - jnp.max(s, axis=-1, keepdims=True)).astype(BF16)


def _gqa_kernel(q_ref, k_ref, v_ref, o_ref):
    low = _lane_ids((q_ref.shape[0], LANES)) < HEAD_DIM
    for pair in range(q_ref.shape[1] // LANES):
        kv = pair // (A_HEADS // A_KV_HEADS // 2)
        k = k_ref[:, kv * LANES:(kv + 1) * LANES]
        q = q_ref[:, pair * LANES:(pair + 1) * LANES]
        zero = jnp.zeros_like(q)
        normed = []
        for half, qh in enumerate((jnp.where(low, q, zero), jnp.where(low, zero, q))):
            e = _exp_scores(lax.dot_general(qh, k, _NT, preferred_element_type=F32))
            v = v_ref[:, (2 * kv + half) * LANES:(2 * kv + half + 1) * LANES]
            ol = jnp.dot(e, v, preferred_element_type=F32)
            normed.append(ol / pltpu.roll(ol, HEAD_DIM, 1))
        o_ref[:, pair * LANES:(pair + 1) * LANES] = jnp.where(low, normed[0], normed[1]).astype(
            o_ref.dtype)


def _gqa(qa, ka, va, bsz, seq):
    n = qa.shape[0]
    tq = min(GQA_Q_TILE, seq)
    nq = seq // tq
    kvps = GQA_KV_PER_STEP
    group_width = kvps * A_WIDTH // A_KV_HEADS
    return pl.pallas_call(
        _gqa_kernel,
        grid=(bsz, A_KV_HEADS // kvps, nq),
        in_specs=[pl.BlockSpec((tq, group_width), lambda b, j, t: (b * nq + t, j)),
                  pl.BlockSpec((seq, kvps * LANES), lambda b, j, t: (b, j)),
                  pl.BlockSpec((seq, kvps * 2 * LANES), lambda b, j, t: (b, j))],
        out_specs=pl.BlockSpec((tq, group_width), lambda b, j, t: (b * nq + t, j)),
        out_shape=jax.ShapeDtypeStruct((n, A_WIDTH), BF16),
        compiler_params=_params("arbitrary", "arbitrary", "arbitrary"),
        name="gqa_attention",
    )(qa, ka, va)


def _diff_kernel(lam_init, nq, seq, q_ref, k_ref, v_ref, bias_ref, lq1_ref, lk1_ref, lq2_ref,
                 lk2_ref, sub_ref, o_ref):
    tq = q_ref.shape[0]
    t = pl.program_id(2)
    lam = (jnp.exp(jnp.sum(lq1_ref[...] * lk1_ref[...], axis=-1, keepdims=True))
           - jnp.exp(jnp.sum(lq2_ref[...] * lk2_ref[...], axis=-1, keepdims=True)) + lam_init)
    low = _lane_ids((tq, LANES)) < HEAD_DIM
    start = pl.multiple_of((nq - 1 - t) * tq, LANES)
    for hd in range(DIFF_HEADS_PER_STEP):
        q = q_ref[:, hd * LANES:(hd + 1) * LANES]
        k = k_ref[:, hd * LANES:(hd + 1) * LANES]
        v = v_ref[:, 2 * hd * B_V_DIM:(2 * hd + 2) * B_V_DIM]
        zero = jnp.zeros_like(q)
        bias = bias_ref[hd, :, pl.ds(start, seq)]
        parts = []
        for qh in (jnp.where(low, q, zero), jnp.where(low, zero, q)):
            e = _exp_scores(lax.dot_general(qh, k, _NT, preferred_element_type=F32) + bias)
            ol = jnp.dot(e, v, preferred_element_type=F32)
            parts.append(ol[:, 0:B_V_DIM] / ol[:, B_V_DIM:2 * B_V_DIM])
        o = parts[0] - lam * parts[1]
        o_ref[:, hd * B_V_DIM:(hd + 1) * B_V_DIM] = (
            _rms(o) * sub_ref[...] * (1.0 - lam_init)).astype(o_ref.dtype)


def _diff(qb, kb, vb, bias_t, lq1, lk1, lq2, lk2, subln, lam_init, bsz, seq):
    n = qb.shape[0]
    tq = bias_t.shape[1]
    nq = seq // tq
    hps = DIFF_HEADS_PER_STEP

    def small(a):
        return pl.BlockSpec(a.shape, lambda h, b, t: (0, 0))

    return pl.pallas_call(
        functools.partial(_diff_kernel, lam_init, nq, seq),
        grid=(B_HEADS // hps, bsz, nq),
        in_specs=[pl.BlockSpec((tq, hps * LANES), lambda h, b, t: (b * nq + t, h)),
                  pl.BlockSpec((seq, hps * LANES), lambda h, b, t: (b, h)),
                  pl.BlockSpec((seq, hps * 2 * B_V_DIM), lambda h, b, t: (b, h)),
                  pl.BlockSpec((hps, tq, bias_t.shape[2]), lambda h, b, t: (h, 0, 0),
                               pipeline_mode=pl.Buffered(1)),
                  small(lq1), small(lk1), small(lq2), small(lk2), small(subln)],
        out_specs=pl.BlockSpec((tq, hps * B_V_DIM), lambda h, b, t: (b * nq + t, h)),
        out_shape=jax.ShapeDtypeStruct((n, B_WIDTH), BF16),
        compiler_params=_params("arbitrary", "arbitrary", "arbitrary"),
        name="diff_attention",
    )(qb, kb, vb, bias_t, lq1, lk1, lq2, lk2, subln)


def _load_token_major(ref):
    rows = ref.shape[0] // ROW_CHUNKS
    return jnp.concatenate([ref[pl.ds(c, rows, stride=ROW_CHUNKS), :] for c in range(ROW_CHUNKS)],
                           axis=-1)


def _store_token_major(ref, val):
    rows = val.shape[0]
    for c in range(ROW_CHUNKS):
        ref[pl.ds(c, rows, stride=ROW_CHUNKS), :] = val[:, c * LANES:(c + 1) * LANES]


def _route(biased):
    epg = EXPERTS_PER_GROUP
    group_score = []
    for g in range(N_GROUPS):
        v = biased[g * epg:(g + 1) * epg]
        best = None
        for a in range(epg):
            for b in range(a + 1, epg):
                pair = v[a] + v[b]
                best = pair if best is None else jnp.maximum(best, pair)
        group_score.append(best)
    sel = jnp.zeros_like(group_score[0], dtype=jnp.int32)
    top = group_score[0]
    for g in range(1, N_GROUPS):
        better = group_score[g] > top
        sel = jnp.where(better, g, sel)
        top = jnp.where(better, group_score[g], top)

    def pick(vals, k):
        out = vals[k]
        for g in range(1, N_GROUPS):
            out = jnp.where(sel == g, vals[g * epg + k], out)
        return out

    bsel = [pick(biased, k) for k in range(epg)]
    neg = jnp.full_like(bsel[0], -jnp.inf)

    def argmax_first(vals):
        idx = jnp.zeros_like(sel)
        top_v = vals[0]
        for k in range(1, epg):
            better = vals[k] > top_v
            idx = jnp.where(better, k, idx)
            top_v = jnp.where(better, vals[k], top_v)
        return idx

    i1 = argmax_first(bsel)
    i2 = argmax_first([jnp.where(i1 == k, neg, bsel[k]) for k in range(epg)])

    k_lo = jnp.minimum(i1, i2)
    k_hi = jnp.maximum(i1, i2)
    pair = jnp.where(k_lo == 0, k_hi - 1, jnp.where(k_lo == 1, k_hi + 1, PAIRS_PER_GROUP - 1))
    return sel * PAIRS_PER_GROUP + pair


def _merge_kernel(oa_ref, ob_ref, g_ref, x_ref, gate1_ref, shift_ref, scale_ref, nf_ref, wa_ref,
                  wb_ref, wo_ref, wr_ref, tri_ref, rb_ref, x2_ref, hx_ref, cls_ref, rank_ref,
                  count_ref, carry_ref):
    @pl.when(pl.program_id(0) == 0)
    def _():
        carry_ref[...] = jnp.zeros_like(carry_ref)

    ya = jnp.dot(oa_ref[...], wa_ref[...], preferred_element_type=F32)
    yb = jnp.dot(ob_ref[...], wb_ref[...], preferred_element_type=F32)
    m = g_ref[:, 0:D_MODEL].astype(F32) * ya + g_ref[:, D_MODEL:2 * D_MODEL].astype(F32) * yb
    merged = jnp.dot(m.astype(BF16), wo_ref[...], preferred_element_type=F32)
    x2 = x_ref[...] + gate1_ref[0] * merged
    x2_ref[...] = x2
    h2 = _rms(x2) * nf_ref[...] * (1.0 + scale_ref[0]) + shift_ref[0]
    _store_token_major(hx_ref, h2)

    h_hi = h2.astype(BF16)
    h_lo = (h2 - h_hi.astype(F32)).astype(BF16)
    by_hi = jnp.dot(h_hi, wr_ref[...], preferred_element_type=F32)
    logits = (by_hi[:, 0:LANES] + by_hi[:, LANES:2 * LANES]
              + jnp.dot(h_lo, wr_ref[:, 0:LANES], preferred_element_type=F32))
    scores_t = jnp.transpose(jax.nn.sigmoid(logits))
    biased = [scores_t[e:e + 1, :] + rb_ref[e] for e in range(N_EXPERTS)]
    cls = _route(biased)
    rows = scores_t.shape[1]

    class_row = lax.broadcasted_iota(jnp.int32, (CLASS_ROWS, rows), 0)
    onehot = class_row == cls
    before = jnp.dot(jnp.where(onehot, 1.0, 0.0).astype(BF16), tri_ref[...],
                     preferred_element_type=F32)
    carry = carry_ref[...]
    rank = jnp.sum(jnp.where(onehot, before + carry[:, 0:1], 0.0), axis=0, keepdims=True)
    carry = carry + jnp.sum(jnp.where(onehot, 1.0, 0.0), axis=1, keepdims=True)
    carry_ref[...] = carry
    count_ref[...] = carry
    cls_ref[0] = cls
    rank_ref[0] = rank.astype(jnp.int32)


def _merge(oa, ob, g, x, mod, norm_ffn, wa_b, wb_b, wo_b, wr_pad, tri, router_bias, seq):
    n = x.shape[0]
    tm = min(ROW_TILE, seq)
    tps = seq // tm
    nt = n // tm

    def row(width):
        return pl.BlockSpec((tm, width), lambda i: (i, 0))

    def full(a):
        return pl.BlockSpec(a.shape, lambda i: (0,) * a.ndim, pipeline_mode=pl.Buffered(1))

    def modspec(k):
        return pl.BlockSpec((1, 1, D_MODEL), lambda i: (N_MOD * (i // tps) + k, 0, 0))

    lane_vec = pl.BlockSpec((1, 1, tm), lambda i: (i, 0, 0))
    return pl.pallas_call(
        _merge_kernel,
        grid=(nt,),
        in_specs=[row(A_WIDTH), row(B_WIDTH), row(GATE_WIDTH), row(D_MODEL), modspec(2),
                  modspec(3), modspec(4), full(norm_ffn), full(wa_b), full(wb_b), full(wo_b),
                  full(wr_pad), full(tri), pl.BlockSpec(memory_space=pltpu.SMEM)],
        out_specs=[row(D_MODEL), pl.BlockSpec((tm * ROW_CHUNKS, LANES), lambda i: (i, 0)),
                   lane_vec, lane_vec, pl.BlockSpec((CLASS_ROWS, LANES), lambda i: (0, 0))],
        out_shape=[jax.ShapeDtypeStruct((n, D_MODEL), F32),
                   jax.ShapeDtypeStruct((n * ROW_CHUNKS, LANES), F32),
                   jax.ShapeDtypeStruct((nt, 1, tm), jnp.int32),
                   jax.ShapeDtypeStruct((nt, 1, tm), jnp.int32),
                   jax.ShapeDtypeStruct((CLASS_ROWS, LANES), F32)],
        scratch_shapes=[pltpu.VMEM((CLASS_ROWS, LANES), F32)],
        compiler_params=_params("arbitrary"),
        name="merge_route",
    )(oa, ob, g, x, mod, mod, mod, norm_ffn, wa_b, wb_b, wo_b, wr_pad, tri, router_bias)


def _row_copy(src_ref, src_row, dst_ref, dst_row, sem):
    return pltpu.make_async_copy(src_ref.at[src_row], dst_ref.at[dst_row], sem)


def _permute_rows(to_slot, slot_ref, tok_ref, slot_major_ref, sem):
    rows = tok_ref.shape[0]

    def body(j, carry):
        first = j * DMA_UNROLL
        slots = [slot_ref[first + k] for k in range(DMA_UNROLL)]
        for k in range(DMA_UNROLL):
            if to_slot:
                _row_copy(tok_ref, first + k, slot_major_ref, slots[k], sem).start(priority=k % 2)
            else:
                _row_copy(slot_major_ref, slots[k], tok_ref, first + k, sem).start(priority=k % 2)
        return carry

    lax.fori_loop(0, rows // DMA_UNROLL, body, 0)
    pltpu.make_async_copy(slot_major_ref.at[pl.ds(0, rows)], tok_ref, sem).wait()


def _dispatch_kernel(n_tiles, slot_ref, first_ref, tiles_ref, total_ref, src_ref, dst_ref,
                     zero_ref, sem, zero_sem):
    @pl.when(pl.program_id(0) == 0)
    def _():
        zero_ref[...] = jnp.zeros_like(zero_ref)

        def zero_tile(tile):
            return pltpu.make_async_copy(
                zero_ref, dst_ref.at[pl.ds(tile * MOE_TILE, MOE_TILE)], zero_sem)

        def start_tail(tile, carry):
            zero_tile(tile).start()
            return carry

        def wait_one(tile, carry):
            zero_tile(0).wait()
            return carry

        for c in range(N_CLASSES):
            @pl.when(tiles_ref[c] > 0)
            def _():
                zero_tile(first_ref[c] + tiles_ref[c] - 1).start()
        lax.fori_loop(total_ref[0], n_tiles, start_tail, 0)
        for c in range(N_CLASSES):
            @pl.when(tiles_ref[c] > 0)
            def _():
                zero_tile(0).wait()
        lax.fori_loop(total_ref[0], n_tiles, wait_one, 0)

    _permute_rows(True, slot_ref, src_ref, dst_ref, sem)


def _dispatch(hx, slot, plan):
    n = hx.shape[0]
    rows = min(DMA_ROWS, n)
    n_tiles = plan["n_tiles"]
    smem = pl.BlockSpec(memory_space=pltpu.SMEM)
    return pl.pallas_call(
        functools.partial(_dispatch_kernel, n_tiles),
        grid=(n // rows,),
        in_specs=[pl.BlockSpec((rows,), lambda i: (i,), memory_space=pltpu.SMEM), smem, smem, smem,
                  pl.BlockSpec((rows,) + hx.shape[1:], lambda i: (i, 0, 0))],
        out_specs=pl.BlockSpec(memory_space=pl.ANY),
        out_shape=jax.ShapeDtypeStruct((n_tiles * MOE_TILE,) + hx.shape[1:], hx.dtype),
        scratch_shapes=[pltpu.VMEM((MOE_TILE,) + hx.shape[1:], hx.dtype),
                        pltpu.SemaphoreType.DMA(()), pltpu.SemaphoreType.DMA(())],
        compiler_params=_params("arbitrary"),
        name="dispatch_rows",
    )(slot, plan["first_tile"], plan["tiles"], plan["total"], hx)


def _combine_kernel(slot_ref, src_ref, dst_ref, sem):
    _permute_rows(False, slot_ref, dst_ref, src_ref, sem)


def _combine(ys, slot, n):
    rows = min(DMA_ROWS, n)
    return pl.pallas_call(
        _combine_kernel,
        grid=(n // rows,),
        in_specs=[pl.BlockSpec((rows,), lambda i: (i,), memory_space=pltpu.SMEM),
                  pl.BlockSpec(memory_space=pl.ANY)],
        out_specs=pl.BlockSpec((rows,) + ys.shape[1:], lambda i: (i, 0, 0)),
        out_shape=jax.ShapeDtypeStruct((n,) + ys.shape[1:], ys.dtype),
        scratch_shapes=[pltpu.SemaphoreType.DMA(())],
        compiler_params=_params("arbitrary"),
        name="combine_rows",
    )(slot, ys)


def _moe_kernel(lo_ref, hi_ref, valid_ref, hs_ref, wr_ref, wg_lo, wu_lo, wd_lo, wg_hi, wu_hi,
                wd_hi, o_ref):
    t = pl.program_id(0)
    valid = valid_ref[t] > 0

    @pl.when(jnp.logical_not(valid))
    def _():
        o_ref[...] = jnp.zeros_like(o_ref)

    @pl.when(valid)
    def _():
        h32 = _load_token_major(hs_ref)
        h = h32.astype(BF16)

        def ffn(wg_ref, wu_ref, wd_ref):
            a = jnp.dot(h, wg_ref[0], preferred_element_type=F32)
            u = jnp.dot(h, wu_ref[0], preferred_element_type=F32)
            act = (a * jax.nn.sigmoid(a) * u).astype(BF16)
            return jnp.dot(act, wd_ref[0], preferred_element_type=F32)

        def score(e):
            return jax.nn.sigmoid(jnp.sum(h32 * wr_ref[pl.ds(e, 1), :], axis=-1, keepdims=True))

        s_lo = score(lo_ref[t])
        s_hi = score(hi_ref[t])
        denom = s_lo + s_hi
        y = ((s_lo / denom) * ffn(wg_lo, wu_lo, wd_lo)
             + (s_hi / denom) * ffn(wg_hi, wu_hi, wd_hi))
        _store_token_major(o_ref, y)


def _moe(hs, tile_lo, tile_hi, tile_valid, wr_t, wg_b, wu_b, wd_b):
    n_tiles = tile_lo.shape[0]
    up = (1, D_MODEL, D_FF_EXPERT)
    down = (1, D_FF_EXPERT, D_MODEL)

    def by(sel):
        return lambda t, lo, hi, valid: ((lo, hi)[sel][t], 0, 0)

    rows = pl.BlockSpec((MOE_TILE * ROW_CHUNKS, LANES), lambda t, lo, hi, valid: (t, 0))
    return pl.pallas_call(
        _moe_kernel,
        grid_spec=pltpu.PrefetchScalarGridSpec(
            num_scalar_prefetch=3,
            grid=(n_tiles,),
            in_specs=[rows, pl.BlockSpec(wr_t.shape, lambda t, lo, hi, valid: (0, 0)),
                      pl.BlockSpec(up, by(0)), pl.BlockSpec(up, by(0)), pl.BlockSpec(down, by(0)),
                      pl.BlockSpec(up, by(1)), pl.BlockSpec(up, by(1)), pl.BlockSpec(down, by(1))],
            out_specs=rows),
        out_shape=jax.ShapeDtypeStruct(hs.shape, F32),
        compiler_params=_params("arbitrary"),
        name="experts",
    )(tile_lo, tile_hi, tile_valid, hs, wr_t, wg_b, wu_b, wd_b, wg_b, wu_b, wd_b)


def _tile_plan(counts, n):
    n_tiles = n // MOE_TILE + N_CLASSES
    tiles_c = (counts + MOE_TILE - 1) // MOE_TILE
    ends = jnp.cumsum(tiles_c)
    first_tile = ends - tiles_c
    total = ends[-1]
    t = jnp.arange(n_tiles, dtype=jnp.int32)
    last = jnp.maximum(total - 1, 0)
    tile_cls = jnp.sum((jnp.minimum(t, last)[:, None] >= ends[None, :]).astype(jnp.int32), axis=1)
    tile_cls = jnp.minimum(tile_cls, N_CLASSES - 1)
    k_lo, k_hi = np.triu_indices(EXPERTS_PER_GROUP, 1)
    group = np.arange(N_CLASSES) // PAIRS_PER_GROUP
    lo_table = jnp.asarray(group * EXPERTS_PER_GROUP + np.tile(k_lo, N_GROUPS), jnp.int32)
    hi_table = jnp.asarray(group * EXPERTS_PER_GROUP + np.tile(k_hi, N_GROUPS), jnp.int32)
    i32 = jnp.int32
    return {"first_tile": first_tile.astype(i32), "tiles": tiles_c.astype(i32),
            "total": total.astype(i32).reshape(1), "lo": lo_table[tile_cls],
            "hi": hi_table[tile_cls], "valid": (t < total).astype(i32), "n_tiles": n_tiles}


def _final_kernel(x_ref, y_ref, gate2_ref, fn_ref, o_ref):
    o_ref[...] = _rms(x_ref[...] + gate2_ref[0] * _load_token_major(y_ref)) * fn_ref[...]


def _final(x2, y, mod, final_norm, seq):
    n = x2.shape[0]
    tm = min(ROW_TILE, seq)
    tps = seq // tm
    row = pl.BlockSpec((tm, D_MODEL), lambda i: (i, 0))
    return pl.pallas_call(
        _final_kernel,
        grid=(n // tm,),
        in_specs=[row, pl.BlockSpec((tm * ROW_CHUNKS, LANES), lambda i: (i, 0)),
                  pl.BlockSpec((1, 1, D_MODEL), lambda i: (N_MOD * (i // tps) + 5, 0, 0)),
                  pl.BlockSpec(final_norm.shape, lambda i: (0, 0))],
        out_specs=row,
        out_shape=jax.ShapeDtypeStruct((n, D_MODEL), F32),
        compiler_params=_params("arbitrary"),
        name="final_norm",
    )(x2, y, mod, final_norm)


def _rope_tables(seq):
    rows = seq // GRID_W
    row = jnp.repeat(jnp.arange(rows, dtype=F32), GRID_W)
    col = jnp.tile(jnp.arange(GRID_W, dtype=F32), rows)
    inv = ROPE_THETA ** (-jnp.arange(ROT_FREQS, dtype=F32) / ROT_FREQS)
    ang = jnp.stack([row[:, None] * inv, col[:, None] * inv], axis=1)
    cos = jnp.cos(ang)
    sin = jnp.sin(ang)
    cos_h = jnp.concatenate([cos, cos], axis=-1).reshape(seq, HEAD_DIM)
    sin_h = jnp.concatenate([-sin, sin], axis=-1).reshape(seq, HEAD_DIM)
    return jnp.tile(cos_h, (1, 2)), jnp.tile(sin_h, (1, 2))


def _block_ones():
    idx = np.arange(LANES) // HEAD_DIM
    return jnp.asarray((idx[:, None] == idx[None, :]).astype(np.float32), dtype=BF16)


def kernel(x_prompt, x_sample, c_prompt, c_sample, w_ada, b_ada, norm_attn, norm_ffn, w_in,
           q_norm_a, k_norm_a, lambda_q1, lambda_k1, lambda_q2, lambda_k2, subln_b, rel_bias,
           w_branch_a, w_branch_b, w_out, w_router, router_bias, w_gate, w_up, w_down,
           final_norm):
    depth = w_in.shape[0]
    w_in_b = w_in.astype(BF16)
    wa_b = w_branch_a.astype(BF16)
    wb_b = w_branch_b.astype(BF16)
    wo_b = w_out.astype(BF16)
    wg_b = w_gate.astype(BF16)
    wu_b = w_up.astype(BF16)
    wd_b = w_down.astype(BF16)
    wr_hi = w_router.astype(BF16)
    wr_lo = (w_router - wr_hi.astype(F32)).astype(BF16)
    lane_pad = ((0, 0), (0, LANES - N_EXPERTS))
    wr_pad = jnp.concatenate([jnp.pad(wr_hi, lane_pad), jnp.pad(wr_lo, lane_pad)], axis=1)
    wr_t = w_router.T
    bd = _block_ones()
    fn = final_norm.reshape(1, -1)

    def run(x3, c):
        bsz, seq, _ = x3.shape
        tq = min(Q_TILE, seq)
        cos_t, sin_t = _rope_tables(seq)
        bias_t = _bias_table(rel_bias, seq, tq)
        n = bsz * seq
        x = x3.reshape(n, D_MODEL)
        tri = jnp.asarray(np.triu(np.ones((min(ROW_TILE, seq),) * 2, np.float32), 1), BF16)
        delta = None
        prev_mod = None
        for l in range(depth):
            mod = _ada(c, w_ada[l], b_ada[l]).reshape(bsz * N_MOD, 1, D_MODEL)
            outs = _inproj(
                x, delta, mod, prev_mod, norm_attn[l].reshape(1, -1), w_in_b[l],
                jnp.tile(q_norm_a[l], 2).reshape(1, -1), jnp.tile(k_norm_a[l], 2).reshape(1, -1),
                cos_t, sin_t, bd, seq)
            qa, ka, va, qb, kb, vb, g = outs[:7]
            if delta is not None:
                x = outs[7]
            oa = _gqa(qa, ka, va, bsz, seq)
            lam_init = 0.8 - 0.6 * math.exp(-0.3 * l)
            ob = _diff(qb, kb, vb, bias_t, lambda_q1[l].reshape(1, -1), lambda_k1[l].reshape(1, -1),
                       lambda_q2[l].reshape(1, -1), lambda_k2[l].reshape(1, -1),
                       subln_b[l].reshape(1, -1), lam_init, bsz, seq)
            x, hx, cls, rank, counts = _merge(oa, ob, g, x, mod, norm_ffn[l].reshape(1, -1),
                                              wa_b[l], wb_b[l], wo_b[l], wr_pad, tri, router_bias,
                                              seq)
            cls = cls.reshape(n)
            rank = rank.reshape(n)
            plan = _tile_plan(counts[:N_CLASSES, 0].astype(jnp.int32), n)
            n_slots = plan["n_tiles"] * MOE_TILE
            slot = (plan["first_tile"] * MOE_TILE)[cls] + rank
            hs = _dispatch(hx.reshape(n, ROW_CHUNKS, LANES), slot, plan)
            ys = _moe(hs.reshape(n_slots * ROW_CHUNKS, LANES), plan["lo"], plan["hi"],
                      plan["valid"], wr_t, wg_b[l], wu_b[l], wd_b[l])
            delta = _combine(ys.reshape(n_slots, ROW_CHUNKS, LANES), slot, n)
            delta = delta.reshape(n * ROW_CHUNKS, LANES)
            prev_mod = mod
        return _final(x, delta, prev_mod, fn, seq).reshape(bsz, seq, D_MODEL)

    return (run(x_prompt, c_prompt), run(x_sample, c_sample))
```

```python
import functools
import math

import numpy as np
import jax
import jax.numpy as jnp
from jax import lax
from jax.experimental import pallas as pl
from jax.experimental.pallas import tpu as pltpu

F32 = jnp.float32
BF16 = jnp.bfloat16

D_MODEL = 1024
GRID_W = 64
HEAD_DIM = 64
EPS = 1e-6
A_HEADS = 8
A_KV_HEADS = 2
A_WIDTH = A_HEADS * HEAD_DIM
A_KV_WIDTH = A_KV_HEADS * HEAD_DIM
ROT_FREQS = HEAD_DIM // 4
ROPE_THETA = 10000.0
B_HEADS = 4
B_V_DIM = 2 * HEAD_DIM
B_QK_WIDTH = B_HEADS * 2 * HEAD_DIM
B_WIDTH = B_HEADS * B_V_DIM
GATE_WIDTH = 2 * D_MODEL
REL_BUCKETS = 32
REL_MAX_DIST = 128
N_EXPERTS = 16
N_GROUPS = 4
EXPERTS_PER_GROUP = N_EXPERTS // N_GROUPS
D_FF_EXPERT = 512
N_MOD = 6
PAIRS_PER_GROUP = EXPERTS_PER_GROUP * (EXPERTS_PER_GROUP - 1) // 2
N_CLASSES = N_GROUPS * PAIRS_PER_GROUP
CLASS_ROWS = 32
ROW_CHUNKS = D_MODEL // 128

OFF_QA = 0
OFF_KA = OFF_QA + A_WIDTH
OFF_VA = OFF_KA + A_KV_WIDTH
OFF_QB = OFF_VA + A_KV_WIDTH
OFF_KB = OFF_QB + B_QK_WIDTH
OFF_VB = OFF_KB + B_QK_WIDTH
OFF_G = OFF_VB + B_WIDTH
IN_COLS = OFF_G + GATE_WIDTH

LOG2E = math.log2(math.e)
QK_SCALE = HEAD_DIM ** -0.5 * LOG2E

LANES = 128
VMEM_LIMIT = 56 * 1024 * 1024

ROW_TILE = 512
Q_TILE = 256
GQA_Q_TILE = 512
GQA_KV_PER_STEP = 1
GQA_TILES_PER_STEP = 4
DIFF_HEADS_PER_STEP = 4
DIFF_TILES_PER_STEP = 2
BIAS_CHUNK = 256
MOE_TILE = 256
DMA_ROWS = 2048
DMA_UNROLL = 8


def _params(*sem):
    return pltpu.CompilerParams(dimension_semantics=sem, vmem_limit_bytes=VMEM_LIMIT)


def _rms(x, eps=EPS):
    return x * lax.rsqrt(jnp.mean(x * x, axis=-1, keepdims=True) + eps)


def _lane_ids(shape):
    return lax.broadcasted_iota(jnp.int32, shape, len(shape) - 1)


def _ada_kernel(c_ref, w_ref, b_ref, o_ref):
    c = c_ref[...]
    a = (c * jax.nn.sigmoid(c)).astype(BF16)
    o_ref[...] = jnp.dot(a, w_ref[...].astype(BF16), preferred_element_type=F32) + b_ref[...]


def _ada(c, w_ada, b_ada):
    bsz = c.shape[0]
    ncol = w_ada.shape[1] // D_MODEL
    return pl.pallas_call(
        _ada_kernel,
        grid=(ncol,),
        in_specs=[pl.BlockSpec((bsz, D_MODEL), lambda j: (0, 0)),
                  pl.BlockSpec((D_MODEL, D_MODEL), lambda j: (0, j)),
                  pl.BlockSpec((1, D_MODEL), lambda j: (0, j))],
        out_specs=pl.BlockSpec((bsz, D_MODEL), lambda j: (0, j)),
        out_shape=jax.ShapeDtypeStruct((bsz, w_ada.shape[1]), F32),
        compiler_params=_params("arbitrary"),
        name="ada_mod",
    )(c, w_ada, b_ada.reshape(1, -1))


def _bias_kernel(rb_ref, bucket_ref, o_ref):
    bucket = bucket_ref[...]
    accs = [jnp.zeros(bucket.shape, F32) for _ in range(B_HEADS)]
    for b in range(REL_BUCKETS):
        hit = bucket == b
        for h in range(B_HEADS):
            accs[h] = jnp.where(hit, rb_ref[b * B_HEADS + h], accs[h])
    for h in range(B_HEADS):
        o_ref[h] = accs[h] * LOG2E


def _rel_bucket(rel):
    nb = REL_BUCKETS // 2
    max_exact = nb // 2
    ret = jnp.where(rel > 0, nb, 0)
    n = jnp.abs(rel)
    large = max_exact + (jnp.log(jnp.maximum(n, 1).astype(F32) / max_exact)
                         / math.log(REL_MAX_DIST / max_exact) * (nb - max_exact)).astype(jnp.int32)
    large = jnp.minimum(large, nb - 1)
    return ret + jnp.where(n < max_exact, n, large)


def _bias_table(rel_bias, seq, tq):
    nq = seq // tq
    width = seq + (nq - 1) * tq
    r = jnp.arange(tq, dtype=jnp.int32)[:, None]
    c = jnp.arange(width, dtype=jnp.int32)[None, :]
    bucket = _rel_bucket(c - (nq - 1) * tq - r).astype(jnp.int32)
    return pl.pallas_call(
        _bias_kernel,
        grid=(width // BIAS_CHUNK,),
        in_specs=[pl.BlockSpec(memory_space=pltpu.SMEM),
                  pl.BlockSpec((tq, BIAS_CHUNK), lambda j: (0, j))],
        out_specs=pl.BlockSpec((B_HEADS, tq, BIAS_CHUNK), lambda j: (0, 0, j)),
        out_shape=jax.ShapeDtypeStruct((B_HEADS, tq, width), F32),
        compiler_params=_params("arbitrary"),
        name="rel_bias_table",
    )(rel_bias.reshape(-1), bucket)


def _split_dot_ones(sq, bd):
    hi = sq.astype(BF16)
    lo = (sq - hi.astype(F32)).astype(BF16)
    return (jnp.dot(hi, bd, preferred_element_type=F32)
            + jnp.dot(lo, bd, preferred_element_type=F32))


def _head_norm_rope(y, gain, cos, sin, bd):
    ss = _split_dot_ones(y * y, bd)
    yn = y * lax.rsqrt(ss * (1.0 / HEAD_DIM) + EPS) * gain
    lane = _lane_ids(yn.shape)
    partner = jnp.where((lane & ROT_FREQS) == 0,
                        pltpu.roll(yn, LANES - ROT_FREQS, 1),
                        pltpu.roll(yn, ROT_FREQS, 1))
    return yn * cos + partner * sin


def _inproj_kernel(has_delta, *refs):
    if has_delta:
        x_ref, y_ref, gate2_ref = refs[:3]
        refs = refs[3:]
        x = x_ref[...] + gate2_ref[0] * _load_token_major(y_ref)
        xo_ref = refs[-1]
        xo_ref[...] = x
        refs = refs[:-1]
    else:
        x_ref = refs[0]
        refs = refs[1:]
        x = x_ref[...]
    (shift_ref, scale_ref, na_ref, w_ref, qn_ref, kn_ref, cos_ref, sin_ref, bd_ref,
     qa_ref, ka_ref, va_ref, qb_ref, kb_ref, vb_ref, g_ref) = refs
    h = _rms(x) * na_ref[...] * (1.0 + scale_ref[0]) + shift_ref[0]
    hb = h.astype(BF16)

    def seg(lo, width):
        return jnp.dot(hb, w_ref[:, lo:lo + width], preferred_element_type=F32)

    cos = cos_ref[...]
    sin = sin_ref[...]
    bd = bd_ref[...]

    qa = seg(OFF_QA, A_WIDTH)
    for j in range(A_WIDTH // LANES):
        y = _head_norm_rope(qa[:, j * LANES:(j + 1) * LANES], qn_ref[...], cos, sin, bd)
        qa_ref[:, j * LANES:(j + 1) * LANES] = (y * QK_SCALE).astype(BF16)

    lane = _lane_ids((x.shape[0], LANES))
    low = lane < HEAD_DIM
    kva = seg(OFF_KA, 2 * A_KV_WIDTH)
    ka = _head_norm_rope(kva[:, 0:LANES], kn_ref[...], cos, sin, bd)
    ka_sw = pltpu.roll(ka, HEAD_DIM, 1)
    ka_ref[:, 0:LANES] = jnp.where(low, ka, ka_sw).astype(BF16)
    ka_ref[:, LANES:2 * LANES] = jnp.where(low, ka_sw, ka).astype(BF16)
    va = kva[:, LANES:2 * LANES]
    va_sw = pltpu.roll(va, HEAD_DIM, 1)
    va_ref[:, 0:LANES] = jnp.where(low, va, 1.0).astype(BF16)
    va_ref[:, LANES:2 * LANES] = jnp.where(low, 1.0, va_sw).astype(BF16)
    va_ref[:, 2 * LANES:3 * LANES] = jnp.where(low, va_sw, 1.0).astype(BF16)
    va_ref[:, 3 * LANES:4 * LANES] = jnp.where(low, 1.0, va).astype(BF16)

    qb_ref[...] = (seg(OFF_QB, B_QK_WIDTH) * QK_SCALE).astype(BF16)
    kb_ref[...] = seg(OFF_KB, B_QK_WIDTH).astype(BF16)
    ones = jnp.ones((x.shape[0], B_V_DIM), BF16)
    vb = seg(OFF_VB, B_WIDTH).astype(BF16)
    for hd in range(B_HEADS):
        vb_ref[:, 2 * hd * B_V_DIM:(2 * hd + 1) * B_V_DIM] = vb[:, hd * B_V_DIM:(hd + 1) * B_V_DIM]
        vb_ref[:, (2 * hd + 1) * B_V_DIM:(2 * hd + 2) * B_V_DIM] = ones
    chunk = 512
    for j in range(GATE_WIDTH // chunk):
        g_ref[:, j * chunk:(j + 1) * chunk] = jax.nn.sigmoid(seg(OFF_G + j * chunk, chunk)).astype(BF16)


def _inproj(x, delta, mod, prev_mod, norm_attn, w_in_b, qn, kn, cos_t, sin_t, bd, seq):
    n = x.shape[0]
    tm = min(ROW_TILE, seq)
    tps = seq // tm
    has_delta = delta is not None

    def row(width):
        return pl.BlockSpec((tm, width), lambda i: (i, 0))

    def full(a):
        return pl.BlockSpec(a.shape, lambda i: (0,) * a.ndim, pipeline_mode=pl.Buffered(1))

    def modspec(k):
        return pl.BlockSpec((1, 1, D_MODEL), lambda i: (N_MOD * (i // tps) + k, 0, 0))

    def row_out(width, dtype):
        return row(width), jax.ShapeDtypeStruct((n, width), dtype)

    outs = [row_out(A_WIDTH, BF16), row_out(2 * LANES, BF16), row_out(4 * LANES, BF16),
            row_out(B_QK_WIDTH, BF16), row_out(B_QK_WIDTH, BF16), row_out(2 * B_WIDTH, BF16),
            row_out(GATE_WIDTH, BF16)]
    args = [x]
    in_specs = [row(D_MODEL)]
    if has_delta:
        args += [delta, prev_mod]
        in_specs += [pl.BlockSpec((tm * ROW_CHUNKS, LANES), lambda i: (i, 0)), modspec(5)]
        outs = outs + [row_out(D_MODEL, F32)]
    args += [mod, mod, norm_attn, w_in_b, qn, kn, cos_t, sin_t, bd]
    in_specs += [modspec(0), modspec(1), full(norm_attn), full(w_in_b), full(qn), full(kn),
                 pl.BlockSpec((tm, LANES), lambda i: (i % tps, 0)),
                 pl.BlockSpec((tm, LANES), lambda i: (i % tps, 0)),
                 full(bd)]
    return pl.pallas_call(
        functools.partial(_inproj_kernel, has_delta),
        grid=(n // tm,),
        in_specs=in_specs,
        out_specs=[spec for spec, _ in outs],
        out_shape=[shape for _, shape in outs],
        compiler_params=_params("arbitrary"),
        name="in_proj",
    )(*args)


_NT = (((1,), (1,)), ((), ()))


def _exp_scores(s):
    return jnp.exp2(s - jnp.max(s, axis=-1, keepdims=True)).astype(BF16)


def _gqa_kernel(tq, q_ref, k_ref, v_ref, o_ref):
    low = _lane_ids((tq, LANES)) < HEAD_DIM
    for tile in range(q_ref.shape[0] // tq):
        rows = slice(tile * tq, (tile + 1) * tq)
        for pair in range(q_ref.shape[1] // LANES):
            kv = pair // (A_HEADS // A_KV_HEADS // 2)
            k = k_ref[:, kv * LANES:(kv + 1) * LANES]
            q = q_ref[rows, pair * LANES:(pair + 1) * LANES]
            zero = jnp.zeros_like(q)
            normed = []
            for half, qh in enumerate((jnp.where(low, q, zero), jnp.where(low, zero, q))):
                e = _exp_scores(lax.dot_general(qh, k, _NT, preferred_element_type=F32))
                v = v_ref[:, (2 * kv + half) * LANES:(2 * kv + half + 1) * LANES]
                ol = jnp.dot(e, v, preferred_element_type=F32)
                normed.append(ol / pltpu.roll(ol, HEAD_DIM, 1))
            o_ref[rows, pair * LANES:(pair + 1) * LANES] = jnp.where(
                low, normed[0], normed[1]).astype(o_ref.dtype)


def _gqa(qa, ka, va, bsz, seq):
    n = qa.shape[0]
    tq = min(GQA_Q_TILE * GQA_TILES_PER_STEP, seq)
    nq = seq // tq
    kvps = GQA_KV_PER_STEP
    group_width = kvps * A_WIDTH // A_KV_HEADS
    return pl.pallas_call(
        functools.partial(_gqa_kernel, min(GQA_Q_TILE, seq)),
        grid=(bsz, A_KV_HEADS // kvps, nq),
        in_specs=[pl.BlockSpec((tq, group_width), lambda b, j, t: (b * nq + t, j)),
                  pl.BlockSpec((seq, kvps * LANES), lambda b, j, t: (b, j)),
                  pl.BlockSpec((seq, kvps * 2 * LANES), lambda b, j, t: (b, j))],
        out_specs=pl.BlockSpec((tq, group_width), lambda b, j, t: (b * nq + t, j)),
        out_shape=jax.ShapeDtypeStruct((n, A_WIDTH), BF16),
        compiler_params=_params("arbitrary", "arbitrary", "arbitrary"),
        name="gqa_attention",
    )(qa, ka, va)


def _diff_kernel(lam_init, nq, seq, q_ref, k_ref, v_ref, bias_ref, lq1_ref, lk1_ref, lq2_ref,
                 lk2_ref, sub_ref, o_ref):
    tq = bias_ref.shape[1]
    tiles = q_ref.shape[0] // tq
    lam = (jnp.exp(jnp.sum(lq1_ref[...] * lk1_ref[...], axis=-1, keepdims=True))
           - jnp.exp(jnp.sum(lq2_ref[...] * lk2_ref[...], axis=-1, keepdims=True)) + lam_init)
    low = _lane_ids((tq, LANES)) < HEAD_DIM
    for tile in range(tiles):
        rows = slice(tile * tq, (tile + 1) * tq)
        t = pl.program_id(2) * tiles + tile
        start = pl.multiple_of((nq - 1 - t) * tq, LANES)
        for hd in range(DIFF_HEADS_PER_STEP):
            q = q_ref[rows, hd * LANES:(hd + 1) * LANES]
            k = k_ref[:, hd * LANES:(hd + 1) * LANES]
            v = v_ref[:, 2 * hd * B_V_DIM:(2 * hd + 2) * B_V_DIM]
            zero = jnp.zeros_like(q)
            bias = bias_ref[hd, :, pl.ds(start, seq)]
            parts = []
            for qh in (jnp.where(low, q, zero), jnp.where(low, zero, q)):
                e = _exp_scores(lax.dot_general(qh, k, _NT, preferred_element_type=F32) + bias)
                ol = jnp.dot(e, v, preferred_element_type=F32)
                parts.append(ol[:, 0:B_V_DIM] / ol[:, B_V_DIM:2 * B_V_DIM])
            o = parts[0] - lam * parts[1]
            o_ref[rows, hd * B_V_DIM:(hd + 1) * B_V_DIM] = (
                _rms(o) * sub_ref[...] * (1.0 - lam_init)).astype(o_ref.dtype)


def _diff(qb, kb, vb, bias_t, lq1, lk1, lq2, lk2, subln, lam_init, bsz, seq):
    n = qb.shape[0]
    tq = bias_t.shape[1]
    nq = seq // tq
    hps = DIFF_HEADS_PER_STEP
    rows = tq * min(DIFF_TILES_PER_STEP, nq)
    steps = seq // rows

    def small(a):
        return pl.BlockSpec(a.shape, lambda h, b, t: (0, 0))

    return pl.pallas_call(
        functools.partial(_diff_kernel, lam_init, nq, seq),
        grid=(B_HEADS // hps, bsz, steps),
        in_specs=[pl.BlockSpec((rows, hps * LANES), lambda h, b, t: (b * steps + t, h)),
                  pl.BlockSpec((seq, hps * LANES), lambda h, b, t: (b, h)),
                  pl.BlockSpec((seq, hps * 2 * B_V_DIM), lambda h, b, t: (b, h)),
                  pl.BlockSpec((hps, tq, bias_t.shape[2]), lambda h, b, t: (h, 0, 0),
                               pipeline_mode=pl.Buffered(1)),
                  small(lq1), small(lk1), small(lq2), small(lk2), small(subln)],
        out_specs=pl.BlockSpec((rows, hps * B_V_DIM), lambda h, b, t: (b * steps + t, h)),
        out_shape=jax.ShapeDtypeStruct((n, B_WIDTH), BF16),
        compiler_params=_params("arbitrary", "arbitrary", "arbitrary"),
        name="diff_attention",
    )(qb, kb, vb, bias_t, lq1, lk1, lq2, lk2, subln)


def _load_token_major(ref):
    rows = ref.shape[0] // ROW_CHUNKS
    return jnp.concatenate([ref[pl.ds(c, rows, stride=ROW_CHUNKS), :] for c in range(ROW_CHUNKS)],
                           axis=-1)


def _store_token_major(ref, val):
    rows = val.shape[0]
    for c in range(ROW_CHUNKS):
        ref[pl.ds(c, rows, stride=ROW_CHUNKS), :] = val[:, c * LANES:(c + 1) * LANES]


def _route(biased):
    epg = EXPERTS_PER_GROUP
    group_score = []
    for g in range(N_GROUPS):
        v = biased[g * epg:(g + 1) * epg]
        best = None
        for a in range(epg):
            for b in range(a + 1, epg):
                pair = v[a] + v[b]
                best = pair if best is None else jnp.maximum(best, pair)
        group_score.append(best)
    sel = jnp.zeros_like(group_score[0], dtype=jnp.int32)
    top = group_score[0]
    for g in range(1, N_GROUPS):
        better = group_score[g] > top
        sel = jnp.where(better, g, sel)
        top = jnp.where(better, group_score[g], top)

    def pick(vals, k):
        out = vals[k]
        for g in range(1, N_GROUPS):
            out = jnp.where(sel == g, vals[g * epg + k], out)
        return out

    bsel = [pick(biased, k) for k in range(epg)]
    neg = jnp.full_like(bsel[0], -jnp.inf)

    def argmax_first(vals):
        idx = jnp.zeros_like(sel)
        top_v = vals[0]
        for k in range(1, epg):
            better = vals[k] > top_v
            idx = jnp.where(better, k, idx)
            top_v = jnp.where(better, vals[k], top_v)
        return idx

    i1 = argmax_first(bsel)
    i2 = argmax_first([jnp.where(i1 == k, neg, bsel[k]) for k in range(epg)])

    k_lo = jnp.minimum(i1, i2)
    k_hi = jnp.maximum(i1, i2)
    pair = jnp.where(k_lo == 0, k_hi - 1, jnp.where(k_lo == 1, k_hi + 1, PAIRS_PER_GROUP - 1))
    return sel * PAIRS_PER_GROUP + pair


def _merge_kernel(oa_ref, ob_ref, g_ref, x_ref, gate1_ref, shift_ref, scale_ref, nf_ref, wa_ref,
                  wb_ref, wo_ref, wr_ref, tri_ref, rb_ref, x2_ref, hx_ref, cls_ref, rank_ref,
                  count_ref, carry_ref):
    @pl.when(pl.program_id(0) == 0)
    def _():
        carry_ref[...] = jnp.zeros_like(carry_ref)

    ya = jnp.dot(oa_ref[...], wa_ref[...], preferred_element_type=F32)
    yb = jnp.dot(ob_ref[...], wb_ref[...], preferred_element_type=F32)
    m = g_ref[:, 0:D_MODEL].astype(F32) * ya + g_ref[:, D_MODEL:2 * D_MODEL].astype(F32) * yb
    merged = jnp.dot(m.astype(BF16), wo_ref[...], preferred_element_type=F32)
    x2 = x_ref[...] + gate1_ref[0] * merged
    x2_ref[...] = x2
    h2 = _rms(x2) * nf_ref[...] * (1.0 + scale_ref[0]) + shift_ref[0]
    _store_token_major(hx_ref, h2)

    h_hi = h2.astype(BF16)
    h_lo = (h2 - h_hi.astype(F32)).astype(BF16)
    by_hi = jnp.dot(h_hi, wr_ref[...], preferred_element_type=F32)
    logits = (by_hi[:, 0:LANES] + by_hi[:, LANES:2 * LANES]
              + jnp.dot(h_lo, wr_ref[:, 0:LANES], preferred_element_type=F32))
    scores_t = jnp.transpose(jax.nn.sigmoid(logits))
    biased = [scores_t[e:e + 1, :] + rb_ref[e] for e in range(N_EXPERTS)]
    cls = _route(biased)
    rows = scores_t.shape[1]

    class_row = lax.broadcasted_iota(jnp.int32, (CLASS_ROWS, rows), 0)
    onehot = class_row == cls
    before = jnp.dot(jnp.where(onehot, 1.0, 0.0).astype(BF16), tri_ref[...],
                     preferred_element_type=F32)
    carry = carry_ref[...]
    rank = jnp.sum(jnp.where(onehot, before + carry[:, 0:1], 0.0), axis=0, keepdims=True)
    carry = carry + jnp.sum(jnp.where(onehot, 1.0, 0.0), axis=1, keepdims=True)
    carry_ref[...] = carry
    count_ref[...] = carry
    cls_ref[0] = cls
    rank_ref[0] = rank.astype(jnp.int32)


def _merge(oa, ob, g, x, mod, norm_ffn, wa_b, wb_b, wo_b, wr_pad, tri, router_bias, seq):
    n = x.shape[0]
    tm = min(ROW_TILE, seq)
    tps = seq // tm
    nt = n // tm

    def row(width):
        return pl.BlockSpec((tm, width), lambda i: (i, 0))

    def full(a):
        return pl.BlockSpec(a.shape, lambda i: (0,) * a.ndim, pipeline_mode=pl.Buffered(1))

    def modspec(k):
        return pl.BlockSpec((1, 1, D_MODEL), lambda i: (N_MOD * (i // tps) + k, 0, 0))

    lane_vec = pl.BlockSpec((1, 1, tm), lambda i: (i, 0, 0))
    return pl.pallas_call(
        _merge_kernel,
        grid=(nt,),
        in_specs=[row(A_WIDTH), row(B_WIDTH), row(GATE_WIDTH), row(D_MODEL), modspec(2),
                  modspec(3), modspec(4), full(norm_ffn), full(wa_b), full(wb_b), full(wo_b),
                  full(wr_pad), full(tri), pl.BlockSpec(memory_space=pltpu.SMEM)],
        out_specs=[row(D_MODEL), pl.BlockSpec((tm * ROW_CHUNKS, LANES), lambda i: (i, 0)),
                   lane_vec, lane_vec, pl.BlockSpec((CLASS_ROWS, LANES), lambda i: (0, 0))],
        out_shape=[jax.ShapeDtypeStruct((n, D_MODEL), F32),
                   jax.ShapeDtypeStruct((n * ROW_CHUNKS, LANES), F32),
                   jax.ShapeDtypeStruct((nt, 1, tm), jnp.int32),
                   jax.ShapeDtypeStruct((nt, 1, tm), jnp.int32),
                   jax.ShapeDtypeStruct((CLASS_ROWS, LANES), F32)],
        scratch_shapes=[pltpu.VMEM((CLASS_ROWS, LANES), F32)],
        compiler_params=_params("arbitrary"),
        name="merge_route",
    )(oa, ob, g, x, mod, mod, mod, norm_ffn, wa_b, wb_b, wo_b, wr_pad, tri, router_bias)


def _row_copy(src_ref, src_row, dst_ref, dst_row, sem):
    return pltpu.make_async_copy(src_ref.at[src_row], dst_ref.at[dst_row], sem)


def _permute_rows(to_slot, slot_ref, tok_ref, slot_major_ref, sem):
    rows = tok_ref.shape[0]

    def body(j, carry):
        first = j * DMA_UNROLL
        slots = [slot_ref[first + k] for k in range(DMA_UNROLL)]
        for k in range(DMA_UNROLL):
            if to_slot:
                _row_copy(tok_ref, first + k, slot_major_ref, slots[k], sem).start(priority=k % 2)
            else:
                _row_copy(slot_major_ref, slots[k], tok_ref, first + k, sem).start(priority=k % 2)
        return carry

    lax.fori_loop(0, rows // DMA_UNROLL, body, 0)
    pltpu.make_async_copy(slot_major_ref.at[pl.ds(0, rows)], tok_ref, sem).wait()


def _dispatch_kernel(n_tiles, slot_ref, first_ref, tiles_ref, total_ref, src_ref, dst_ref,
                     zero_ref, sem, zero_sem):
    @pl.when(pl.program_id(0) == 0)
    def _():
        zero_ref[...] = jnp.zeros_like(zero_ref)

        def zero_tile(tile):
            return pltpu.make_async_copy(
                zero_ref, dst_ref.at[pl.ds(tile * MOE_TILE, MOE_TILE)], zero_sem)

        def start_tail(tile, carry):
            zero_tile(tile).start()
            return carry

        def wait_one(tile, carry):
            zero_tile(0).wait()
            return carry

        for c in range(N_CLASSES):
            @pl.when(tiles_ref[c] > 0)
            def _():
                zero_tile(first_ref[c] + tiles_ref[c] - 1).start()
        lax.fori_loop(total_ref[0], n_tiles, start_tail, 0)
        for c in range(N_CLASSES):
            @pl.when(tiles_ref[c] > 0)
            def _():
                zero_tile(0).wait()
        lax.fori_loop(total_ref[0], n_tiles, wait_one, 0)

    _permute_rows(True, slot_ref, src_ref, dst_ref, sem)


def _dispatch(hx, slot, plan):
    n = hx.shape[0]
    rows = min(DMA_ROWS, n)
    n_tiles = plan["n_tiles"]
    smem = pl.BlockSpec(memory_space=pltpu.SMEM)
    return pl.pallas_call(
        functools.partial(_dispatch_kernel, n_tiles),
        grid=(n // rows,),
        in_specs=[pl.BlockSpec((rows,), lambda i: (i,), memory_space=pltpu.SMEM), smem, smem, smem,
                  pl.BlockSpec((rows,) + hx.shape[1:], lambda i: (i, 0, 0))],
        out_specs=pl.BlockSpec(memory_space=pl.ANY),
        out_shape=jax.ShapeDtypeStruct((n_tiles * MOE_TILE,) + hx.shape[1:], hx.dtype),
        scratch_shapes=[pltpu.VMEM((MOE_TILE,) + hx.shape[1:], hx.dtype),
                        pltpu.SemaphoreType.DMA(()), pltpu.SemaphoreType.DMA(())],
        compiler_params=_params("arbitrary"),
        name="dispatch_rows",
    )(slot, plan["first_tile"], plan["tiles"], plan["total"], hx)


def _combine_kernel(slot_ref, src_ref, dst_ref, sem):
    _permute_rows(False, slot_ref, dst_ref, src_ref, sem)


def _combine(ys, slot, n):
    rows = min(DMA_ROWS, n)
    return pl.pallas_call(
        _combine_kernel,
        grid=(n // rows,),
        in_specs=[pl.BlockSpec((rows,), lambda i: (i,), memory_space=pltpu.SMEM),
                  pl.BlockSpec(memory_space=pl.ANY)],
        out_specs=pl.BlockSpec((rows,) + ys.shape[1:], lambda i: (i, 0, 0)),
        out_shape=jax.ShapeDtypeStruct((n,) + ys.shape[1:], ys.dtype),
        scratch_shapes=[pltpu.SemaphoreType.DMA(())],
        compiler_params=_params("arbitrary"),
        name="combine_rows",
    )(slot, ys)


def _moe_kernel(lo_ref, hi_ref, valid_ref, hs_ref, wr_ref, wg_lo, wu_lo, wd_lo, wg_hi, wu_hi,
                wd_hi, o_ref):
    t = pl.program_id(0)
    valid = valid_ref[t] > 0

    @pl.when(jnp.logical_not(valid))
    def _():
        o_ref[...] = jnp.zeros_like(o_ref)

    @pl.when(valid)
    def _():
        h32 = _load_token_major(hs_ref)
        h = h32.astype(BF16)

        def ffn(wg_ref, wu_ref, wd_ref):
            a = jnp.dot(h, wg_ref[0], preferred_element_type=F32)
            u = jnp.dot(h, wu_ref[0], preferred_element_type=F32)
            act = (a * jax.nn.sigmoid(a) * u).astype(BF16)
            return jnp.dot(act, wd_ref[0], preferred_element_type=F32)

        def score(e):
            return jax.nn.sigmoid(jnp.sum(h32 * wr_ref[pl.ds(e, 1), :], axis=-1, keepdims=True))

        s_lo = score(lo_ref[t])
        s_hi = score(hi_ref[t])
        denom = s_lo + s_hi
        y = ((s_lo / denom) * ffn(wg_lo, wu_lo, wd_lo)
             + (s_hi / denom) * ffn(wg_hi, wu_hi, wd_hi))
        _store_token_major(o_ref, y)


def _moe(hs, tile_lo, tile_hi, tile_valid, wr_t, wg_b, wu_b, wd_b):
    n_tiles = tile_lo.shape[0]
    up = (1, D_MODEL, D_FF_EXPERT)
    down = (1, D_FF_EXPERT, D_MODEL)

    def by(sel):
        return lambda t, lo, hi, valid: ((lo, hi)[sel][t], 0, 0)

    rows = pl.BlockSpec((MOE_TILE * ROW_CHUNKS, LANES), lambda t, lo, hi, valid: (t, 0))
    return pl.pallas_call(
        _moe_kernel,
        grid_spec=pltpu.PrefetchScalarGridSpec(
            num_scalar_prefetch=3,
            grid=(n_tiles,),
            in_specs=[rows, pl.BlockSpec(wr_t.shape, lambda t, lo, hi, valid: (0, 0)),
                      pl.BlockSpec(up, by(0)), pl.BlockSpec(up, by(0)), pl.BlockSpec(down, by(0)),
                      pl.BlockSpec(up, by(1)), pl.BlockSpec(up, by(1)), pl.BlockSpec(down, by(1))],
            out_specs=rows),
        out_shape=jax.ShapeDtypeStruct(hs.shape, F32),
        compiler_params=_params("arbitrary"),
        name="experts",
    )(tile_lo, tile_hi, tile_valid, hs, wr_t, wg_b, wu_b, wd_b, wg_b, wu_b, wd_b)


def _tile_plan(counts, n):
    n_tiles = n // MOE_TILE + N_CLASSES
    tiles_c = (counts + MOE_TILE - 1) // MOE_TILE
    ends = jnp.cumsum(tiles_c)
    first_tile = ends - tiles_c
    total = ends[-1]
    t = jnp.arange(n_tiles, dtype=jnp.int32)
    last = jnp.maximum(total - 1, 0)
    tile_cls = jnp.sum((jnp.minimum(t, last)[:, None] >= ends[None, :]).astype(jnp.int32), axis=1)
    tile_cls = jnp.minimum(tile_cls, N_CLASSES - 1)
    k_lo, k_hi = np.triu_indices(EXPERTS_PER_GROUP, 1)
    group = np.arange(N_CLASSES) // PAIRS_PER_GROUP
    lo_table = jnp.asarray(group * EXPERTS_PER_GROUP + np.tile(k_lo, N_GROUPS), jnp.int32)
    hi_table = jnp.asarray(group * EXPERTS_PER_GROUP + np.tile(k_hi, N_GROUPS), jnp.int32)
    i32 = jnp.int32
    return {"first_tile": first_tile.astype(i32), "tiles": tiles_c.astype(i32),
            "total": total.astype(i32).reshape(1), "lo": lo_table[tile_cls],
            "hi": hi_table[tile_cls], "valid": (t < total).astype(i32), "n_tiles": n_tiles}


def _final_kernel(x_ref, y_ref, gate2_ref, fn_ref, o_ref):
    o_ref[...] = _rms(x_ref[...] + gate2_ref[0] * _load_token_major(y_ref)) * fn_ref[...]


def _final(x2, y, mod, final_norm, seq):
    n = x2.shape[0]
    tm = min(ROW_TILE, seq)
    tps = seq // tm
    row = pl.BlockSpec((tm, D_MODEL), lambda i: (i, 0))
    return pl.pallas_call(
        _final_kernel,
        grid=(n // tm,),
        in_specs=[row, pl.BlockSpec((tm * ROW_CHUNKS, LANES), lambda i: (i, 0)),
                  pl.BlockSpec((1, 1, D_MODEL), lambda i: (N_MOD * (i // tps) + 5, 0, 0)),
                  pl.BlockSpec(final_norm.shape, lambda i: (0, 0))],
        out_specs=row,
        out_shape=jax.ShapeDtypeStruct((n, D_MODEL), F32),
        compiler_params=_params("arbitrary"),
        name="final_norm",
    )(x2, y, mod, final_norm)


def _rope_tables(seq):
    rows = seq // GRID_W
    row = jnp.repeat(jnp.arange(rows, dtype=F32), GRID_W)
    col = jnp.tile(jnp.arange(GRID_W, dtype=F32), rows)
    inv = ROPE_THETA ** (-jnp.arange(ROT_FREQS, dtype=F32) / ROT_FREQS)
    ang = jnp.stack([row[:, None] * inv, col[:, None] * inv], axis=1)
    cos = jnp.cos(ang)
    sin = jnp.sin(ang)
    cos_h = jnp.concatenate([cos, cos], axis=-1).reshape(seq, HEAD_DIM)
    sin_h = jnp.concatenate([-sin, sin], axis=-1).reshape(seq, HEAD_DIM)
    return jnp.tile(cos_h, (1, 2)), jnp.tile(sin_h, (1, 2))


def _block_ones():
    idx = np.arange(LANES) // HEAD_DIM
    return jnp.asarray((idx[:, None] == idx[None, :]).astype(np.float32), dtype=BF16)


def kernel(x_prompt, x_sample, c_prompt, c_sample, w_ada, b_ada, norm_attn, norm_ffn, w_in,
           q_norm_a, k_norm_a, lambda_q1, lambda_k1, lambda_q2, lambda_k2, subln_b, rel_bias,
           w_branch_a, w_branch_b, w_out, w_router, router_bias, w_gate, w_up, w_down,
           final_norm):
    depth = w_in.shape[0]
    w_in_b = w_in.astype(BF16)
    wa_b = w_branch_a.astype(BF16)
    wb_b = w_branch_b.astype(BF16)
    wo_b = w_out.astype(BF16)
    wg_b = w_gate.astype(BF16)
    wu_b = w_up.astype(BF16)
    wd_b = w_down.astype(BF16)
    wr_hi = w_router.astype(BF16)
    wr_lo = (w_router - wr_hi.astype(F32)).astype(BF16)
    lane_pad = ((0, 0), (0, LANES - N_EXPERTS))
    wr_pad = jnp.concatenate([jnp.pad(wr_hi, lane_pad), jnp.pad(wr_lo, lane_pad)], axis=1)
    wr_t = w_router.T
    bd = _block_ones()
    fn = final_norm.reshape(1, -1)

    def run(x3, c):
        bsz, seq, _ = x3.shape
        tq = min(Q_TILE, seq)
        cos_t, sin_t = _rope_tables(seq)
        bias_t = _bias_table(rel_bias, seq, tq)
        n = bsz * seq
        x = x3.reshape(n, D_MODEL)
        tri = jnp.asarray(np.triu(np.ones((min(ROW_TILE, seq),) * 2, np.float32), 1), BF16)
        delta = None
        prev_mod = None
        for l in range(depth):
            mod = _ada(c, w_ada[l], b_ada[l]).reshape(bsz * N_MOD, 1, D_MODEL)
            outs = _inproj(
                x, delta, mod, prev_mod, norm_attn[l].reshape(1, -1), w_in_b[l],
                jnp.tile(q_norm_a[l], 2).reshape(1, -1), jnp.tile(k_norm_a[l], 2).reshape(1, -1),
                cos_t, sin_t, bd, seq)
            qa, ka, va, qb, kb, vb, g = outs[:7]
            if delta is not None:
                x = outs[7]
            oa = _gqa(qa, ka, va, bsz, seq)
            lam_init = 0.8 - 0.6 * math.exp(-0.3 * l)
            ob = _diff(qb, kb, vb, bias_t, lambda_q1[l].reshape(1, -1), lambda_k1[l].reshape(1, -1),
                       lambda_q2[l].reshape(1, -1), lambda_k2[l].reshape(1, -1),
                       subln_b[l].reshape(1, -1), lam_init, bsz, seq)
            x, hx, cls, rank, counts = _merge(oa, ob, g, x, mod, norm_ffn[l].reshape(1, -1),
                                              wa_b[l], wb_b[l], wo_b[l], wr_pad, tri, router_bias,
                                              seq)
            cls = cls.reshape(n)
            rank = rank.reshape(n)
            plan = _tile_plan(counts[:N_CLASSES, 0].astype(jnp.int32), n)
            n_slots = plan["n_tiles"] * MOE_TILE
            slot = (plan["first_tile"] * MOE_TILE)[cls] + rank
            hs = _dispatch(hx.reshape(n, ROW_CHUNKS, LANES), slot, plan)
            ys = _moe(hs.reshape(n_slots * ROW_CHUNKS, LANES), plan["lo"], plan["hi"],
                      plan["valid"], wr_t, wg_b[l], wu_b[l], wd_b[l])
            delta = _combine(ys.reshape(n_slots, ROW_CHUNKS, LANES), slot, n)
            delta = delta.reshape(n * ROW_CHUNKS, LANES)
            prev_mod = mod
        return _final(x, delta, prev_mod, fn, seq).reshape(bsz, seq, D_MODEL)

    return (run(x_prompt, c_prompt), run(x_sample, c_sample))
```

```python
import functools
import math

import numpy as np
import jax
import jax.numpy as jnp
from jax import lax
from jax.experimental import pallas as pl
from jax.experimental.pallas import tpu as pltpu

F32 = jnp.float32
BF16 = jnp.bfloat16

D_MODEL = 1024
GRID_W = 64
HEAD_DIM = 64
EPS = 1e-6
A_HEADS = 8
A_KV_HEADS = 2
A_WIDTH = A_HEADS * HEAD_DIM
A_KV_WIDTH = A_KV_HEADS * HEAD_DIM
ROT_FREQS = HEAD_DIM // 4
ROPE_THETA = 10000.0
B_HEADS = 4
B_V_DIM = 2 * HEAD_DIM
B_QK_WIDTH = B_HEADS * 2 * HEAD_DIM
B_WIDTH = B_HEADS * B_V_DIM
GATE_WIDTH = 2 * D_MODEL
REL_BUCKETS = 32
REL_MAX_DIST = 128
N_EXPERTS = 16
N_GROUPS = 4
EXPERTS_PER_GROUP = N_EXPERTS // N_GROUPS
D_FF_EXPERT = 512
N_MOD = 6
PAIRS_PER_GROUP = EXPERTS_PER_GROUP * (EXPERTS_PER_GROUP - 1) // 2
N_CLASSES = N_GROUPS * PAIRS_PER_GROUP
CLASS_ROWS = 32
ROW_CHUNKS = D_MODEL // 128

OFF_QA = 0
OFF_KA = OFF_QA + A_WIDTH
OFF_VA = OFF_KA + A_KV_WIDTH
OFF_QB = OFF_VA + A_KV_WIDTH
OFF_KB = OFF_QB + B_QK_WIDTH
OFF_VB = OFF_KB + B_QK_WIDTH
OFF_G = OFF_VB + B_WIDTH
IN_COLS = OFF_G + GATE_WIDTH

LOG2E = math.log2(math.e)
QK_SCALE = HEAD_DIM ** -0.5 * LOG2E

LANES = 128
VMEM_LIMIT = 56 * 1024 * 1024

ROW_TILE = 512
MERGE_SUBTILES = 1
Q_TILE = 256
GQA_Q_TILE = 512
GQA_KV_PER_STEP = 1
GQA_TILES_PER_STEP = 4
DIFF_HEADS_PER_STEP = 4
DIFF_TILES_PER_STEP = 2
BIAS_CHUNK = 256
MOE_TILE = 256
MOE_TILE_LARGE = 512
MOE_LARGE_MIN_TOKENS = 32768
DMA_ROWS = 2048
DMA_UNROLL = 8


def _params(*sem):
    return pltpu.CompilerParams(dimension_semantics=sem, vmem_limit_bytes=VMEM_LIMIT)


def _rms(x, eps=EPS):
    return x * lax.rsqrt(jnp.mean(x * x, axis=-1, keepdims=True) + eps)


def _lane_ids(shape):
    return lax.broadcasted_iota(jnp.int32, shape, len(shape) - 1)


def _ada_kernel(c_ref, w_ref, b_ref, o_ref):
    c = c_ref[...]
    a = (c * jax.nn.sigmoid(c)).astype(BF16)
    o_ref[...] = jnp.dot(a, w_ref[0].astype(BF16), preferred_element_type=F32) + b_ref[0]


def _ada(c, w_ada, b_ada, layer):
    bsz = c.shape[0]
    ncol = w_ada.shape[2] // D_MODEL
    return pl.pallas_call(
        _ada_kernel,
        grid=(ncol,),
        in_specs=[pl.BlockSpec((bsz, D_MODEL), lambda j: (0, 0)),
                  pl.BlockSpec((1, D_MODEL, D_MODEL), lambda j: (layer, 0, j)),
                  pl.BlockSpec((1, 1, D_MODEL), lambda j: (layer, 0, j))],
        out_specs=pl.BlockSpec((bsz, D_MODEL), lambda j: (0, j)),
        out_shape=jax.ShapeDtypeStruct((bsz, w_ada.shape[2]), F32),
        compiler_params=_params("arbitrary"),
        name="ada_mod",
    )(c, w_ada, b_ada.reshape(b_ada.shape[0], 1, -1))


def _bias_kernel(rb_ref, bucket_ref, o_ref):
    bucket = bucket_ref[...]
    accs = [jnp.zeros(bucket.shape, F32) for _ in range(B_HEADS)]
    for b in range(REL_BUCKETS):
        hit = bucket == b
        for h in range(B_HEADS):
            accs[h] = jnp.where(hit, rb_ref[b * B_HEADS + h], accs[h])
    for h in range(B_HEADS):
        o_ref[h] = accs[h] * LOG2E


def _rel_bucket(rel):
    nb = REL_BUCKETS // 2
    max_exact = nb // 2
    ret = jnp.where(rel > 0, nb, 0)
    n = jnp.abs(rel)
    large = max_exact + (jnp.log(jnp.maximum(n, 1).astype(F32) / max_exact)
                         / math.log(REL_MAX_DIST / max_exact) * (nb - max_exact)).astype(jnp.int32)
    large = jnp.minimum(large, nb - 1)
    return ret + jnp.where(n < max_exact, n, large)


def _bias_table(rel_bias, seq, tq):
    nq = seq // tq
    width = seq + (nq - 1) * tq
    r = jnp.arange(tq, dtype=jnp.int32)[:, None]
    c = jnp.arange(width, dtype=jnp.int32)[None, :]
    bucket = _rel_bucket(c - (nq - 1) * tq - r).astype(jnp.int32)
    return pl.pallas_call(
        _bias_kernel,
        grid=(width // BIAS_CHUNK,),
        in_specs=[pl.BlockSpec(memory_space=pltpu.SMEM),
                  pl.BlockSpec((tq, BIAS_CHUNK), lambda j: (0, j))],
        out_specs=pl.BlockSpec((B_HEADS, tq, BIAS_CHUNK), lambda j: (0, 0, j)),
        out_shape=jax.ShapeDtypeStruct((B_HEADS, tq, width), F32),
        compiler_params=_params("arbitrary"),
        name="rel_bias_table",
    )(rel_bias.reshape(-1), bucket)


def _split_dot_ones(sq, bd):
    hi = sq.astype(BF16)
    lo = (sq - hi.astype(F32)).astype(BF16)
    return (jnp.dot(hi, bd, preferred_element_type=F32)
            + jnp.dot(lo, bd, preferred_element_type=F32))


def _head_norm_rope(y, gain, cos, sin, bd):
    ss = _split_dot_ones(y * y, bd)
    yn = y * lax.rsqrt(ss * (1.0 / HEAD_DIM) + EPS) * gain
    lane = _lane_ids(yn.shape)
    partner = jnp.where((lane & ROT_FREQS) == 0,
                        pltpu.roll(yn, LANES - ROT_FREQS, 1),
                        pltpu.roll(yn, ROT_FREQS, 1))
    return yn * cos + partner * sin


def _inproj_kernel(has_delta, *refs):
    if has_delta:
        x_ref, y_ref, gate2_ref = refs[:3]
        refs = refs[3:]
        x = x_ref[...] + gate2_ref[0] * _load_token_major(y_ref)
        xo_ref = refs[-1]
        xo_ref[...] = x
        refs = refs[:-1]
    else:
        x_ref = refs[0]
        refs = refs[1:]
        x = x_ref[...]
    (shift_ref, scale_ref, na_ref, w_ref, qn_ref, kn_ref, cos_ref, sin_ref, bd_ref,
     qa_ref, ka_ref, va_ref, qb_ref, kb_ref, vb_ref, g_ref) = refs
    h = _rms(x) * na_ref[...] * (1.0 + scale_ref[0]) + shift_ref[0]
    hb = h.astype(BF16)

    def seg(lo, width):
        return jnp.dot(hb, w_ref[:, lo:lo + width], preferred_element_type=F32)

    cos = cos_ref[...]
    sin = sin_ref[...]
    bd = bd_ref[...]

    qa = seg(OFF_QA, A_WIDTH)
    for j in range(A_WIDTH // LANES):
        y = _head_norm_rope(qa[:, j * LANES:(j + 1) * LANES], qn_ref[...], cos, sin, bd)
        qa_ref[:, j * LANES:(j + 1) * LANES] = (y * QK_SCALE).astype(BF16)

    lane = _lane_ids((x.shape[0], LANES))
    low = lane < HEAD_DIM
    kva = seg(OFF_KA, 2 * A_KV_WIDTH)
    ka = _head_norm_rope(kva[:, 0:LANES], kn_ref[...], cos, sin, bd)
    ka_sw = pltpu.roll(ka, HEAD_DIM, 1)
    ka_ref[:, 0:LANES] = jnp.where(low, ka, ka_sw).astype(BF16)
    ka_ref[:, LANES:2 * LANES] = jnp.where(low, ka_sw, ka).astype(BF16)
    va = kva[:, LANES:2 * LANES]
    va_sw = pltpu.roll(va, HEAD_DIM, 1)
    va_ref[:, 0:LANES] = jnp.where(low, va, 1.0).astype(BF16)
    va_ref[:, LANES:2 * LANES] = jnp.where(low, 1.0, va_sw).astype(BF16)
    va_ref[:, 2 * LANES:3 * LANES] = jnp.where(low, va_sw, 1.0).astype(BF16)
    va_ref[:, 3 * LANES:4 * LANES] = jnp.where(low, 1.0, va).astype(BF16)

    qb_ref[...] = (seg(OFF_QB, B_QK_WIDTH) * QK_SCALE).astype(BF16)
    kb_ref[...] = seg(OFF_KB, B_QK_WIDTH).astype(BF16)
    ones = jnp.ones((x.shape[0], B_V_DIM), BF16)
    vb = seg(OFF_VB, B_WIDTH).astype(BF16)
    for hd in range(B_HEADS):
        vb_ref[:, 2 * hd * B_V_DIM:(2 * hd + 1) * B_V_DIM] = vb[:, hd * B_V_DIM:(hd + 1) * B_V_DIM]
        vb_ref[:, (2 * hd + 1) * B_V_DIM:(2 * hd + 2) * B_V_DIM] = ones
    chunk = 512
    for j in range(GATE_WIDTH // chunk):
        g_ref[:, j * chunk:(j + 1) * chunk] = jax.nn.sigmoid(seg(OFF_G + j * chunk, chunk)).astype(BF16)


def _inproj(x, delta, mod, prev_mod, norm_attn, w_in_b, qn, kn, cos_t, sin_t, bd, seq):
    n = x.shape[0]
    tm = min(ROW_TILE, seq)
    tps = seq // tm
    has_delta = delta is not None

    def row(width):
        return pl.BlockSpec((tm, width), lambda i: (i, 0))

    def full(a):
        return pl.BlockSpec(a.shape, lambda i: (0,) * a.ndim, pipeline_mode=pl.Buffered(1))

    def modspec(k):
        return pl.BlockSpec((1, 1, D_MODEL), lambda i: (N_MOD * (i // tps) + k, 0, 0))

    def row_out(width, dtype):
        return row(width), jax.ShapeDtypeStruct((n, width), dtype)

    outs = [row_out(A_WIDTH, BF16), row_out(2 * LANES, BF16), row_out(4 * LANES, BF16),
            row_out(B_QK_WIDTH, BF16), row_out(B_QK_WIDTH, BF16), row_out(2 * B_WIDTH, BF16),
            row_out(GATE_WIDTH, BF16)]
    args = [x]
    in_specs = [row(D_MODEL)]
    if has_delta:
        args += [delta, prev_mod]
        in_specs += [pl.BlockSpec((tm * ROW_CHUNKS, LANES), lambda i: (i, 0)), modspec(5)]
        outs = outs + [row_out(D_MODEL, F32)]
    args += [mod, mod, norm_attn, w_in_b, qn, kn, cos_t, sin_t, bd]
    in_specs += [modspec(0), modspec(1), full(norm_attn), full(w_in_b), full(qn), full(kn),
                 pl.BlockSpec((tm, LANES), lambda i: (i % tps, 0)),
                 pl.BlockSpec((tm, LANES), lambda i: (i % tps, 0)),
                 full(bd)]
    return pl.pallas_call(
        functools.partial(_inproj_kernel, has_delta),
        grid=(n // tm,),
        in_specs=in_specs,
        out_specs=[spec for spec, _ in outs],
        out_shape=[shape for _, shape in outs],
        compiler_params=_params("arbitrary"),
        name="in_proj",
    )(*args)


_NT = (((1,), (1,)), ((), ()))


def _exp_scores(s):
    return jnp.exp2(s - jnp.max(s, axis=-1, keepdims=True)).astype(BF16)


def _gqa_kernel(tq, q_ref, k_ref, v_ref, o_ref):
    low = _lane_ids((tq, LANES)) < HEAD_DIM
    for tile in range(q_ref.shape[0] // tq):
        rows = slice(tile * tq, (tile + 1) * tq)
        for pair in range(q_ref.shape[1] // LANES):
            kv = pair // (A_HEADS // A_KV_HEADS // 2)
            k = k_ref[:, kv * LANES:(kv + 1) * LANES]
            q = q_ref[rows, pair * LANES:(pair + 1) * LANES]
            zero = jnp.zeros_like(q)
            normed = []
            for half, qh in enumerate((jnp.where(low, q, zero), jnp.where(low, zero, q))):
                e = _exp_scores(lax.dot_general(qh, k, _NT, preferred_element_type=F32))
                v = v_ref[:, (2 * kv + half) * LANES:(2 * kv + half + 1) * LANES]
                ol = jnp.dot(e, v, preferred_element_type=F32)
                normed.append(ol / pltpu.roll(ol, HEAD_DIM, 1))
            o_ref[rows, pair * LANES:(pair + 1) * LANES] = jnp.where(
                low, normed[0], normed[1]).astype(o_ref.dtype)


def _gqa(qa, ka, va, bsz, seq):
    n = qa.shape[0]
    tq = min(GQA_Q_TILE * GQA_TILES_PER_STEP, seq)
    nq = seq // tq
    kvps = GQA_KV_PER_STEP
    group_width = kvps * A_WIDTH // A_KV_HEADS
    return pl.pallas_call(
        functools.partial(_gqa_kernel, min(GQA_Q_TILE, seq)),
        grid=(bsz, A_KV_HEADS // kvps, nq),
        in_specs=[pl.BlockSpec((tq, group_width), lambda b, j, t: (b * nq + t, j)),
                  pl.BlockSpec((seq, kvps * LANES), lambda b, j, t: (b, j)),
                  pl.BlockSpec((seq, kvps * 2 * LANES), lambda b, j, t: (b, j))],
        out_specs=pl.BlockSpec((tq, group_width), lambda b, j, t: (b * nq + t, j)),
        out_shape=jax.ShapeDtypeStruct((n, A_WIDTH), BF16),
        compiler_params=_params("arbitrary", "arbitrary", "arbitrary"),
        name="gqa_attention",
    )(qa, ka, va)


def _diff_kernel(lam_init, nq, seq, q_ref, k_ref, v_ref, bias_ref, lq1_ref, lk1_ref, lq2_ref,
                 lk2_ref, sub_ref, o_ref):
    tq = bias_ref.shape[1]
    tiles = q_ref.shape[0] // tq
    lam = (jnp.exp(jnp.sum(lq1_ref[...] * lk1_ref[...], axis=-1, keepdims=True))
           - jnp.exp(jnp.sum(lq2_ref[...] * lk2_ref[...], axis=-1, keepdims=True)) + lam_init)
    low = _lane_ids((tq, LANES)) < HEAD_DIM
    for tile in range(tiles):
        rows = slice(tile * tq, (tile + 1) * tq)
        t = pl.program_id(2) * tiles + tile
        start = pl.multiple_of((nq - 1 - t) * tq, LANES)
        for hd in range(DIFF_HEADS_PER_STEP):
            q = q_ref[rows, hd * LANES:(hd + 1) * LANES]
            k = k_ref[:, hd * LANES:(hd + 1) * LANES]
            v = v_ref[:, 2 * hd * B_V_DIM:(2 * hd + 2) * B_V_DIM]
            zero = jnp.zeros_like(q)
            bias = bias_ref[hd, :, pl.ds(start, seq)]
            parts = []
            for qh in (jnp.where(low, q, zero), jnp.where(low, zero, q)):
                e = _exp_scores(lax.dot_general(qh, k, _NT, preferred_element_type=F32) + bias)
                ol = jnp.dot(e, v, preferred_element_type=F32)
                parts.append(ol[:, 0:B_V_DIM] / ol[:, B_V_DIM:2 * B_V_DIM])
            o = parts[0] - lam * parts[1]
            o_ref[rows, hd * B_V_DIM:(hd + 1) * B_V_DIM] = (
                _rms(o) * sub_ref[...] * (1.0 - lam_init)).astype(o_ref.dtype)


def _diff(qb, kb, vb, bias_t, lq1, lk1, lq2, lk2, subln, lam_init, bsz, seq):
    n = qb.shape[0]
    tq = bias_t.shape[1]
    nq = seq // tq
    hps = DIFF_HEADS_PER_STEP
    rows = tq * min(DIFF_TILES_PER_STEP, nq)
    steps = seq // rows

    def small(a):
        return pl.BlockSpec(a.shape, lambda h, b, t: (0, 0))

    return pl.pallas_call(
        functools.partial(_diff_kernel, lam_init, nq, seq),
        grid=(B_HEADS // hps, bsz, steps),
        in_specs=[pl.BlockSpec((rows, hps * LANES), lambda h, b, t: (b * steps + t, h)),
                  pl.BlockSpec((seq, hps * LANES), lambda h, b, t: (b, h)),
                  pl.BlockSpec((seq, hps * 2 * B_V_DIM), lambda h, b, t: (b, h)),
                  pl.BlockSpec((hps, tq, bias_t.shape[2]), lambda h, b, t: (h, 0, 0),
                               pipeline_mode=pl.Buffered(1)),
                  small(lq1), small(lk1), small(lq2), small(lk2), small(subln)],
        out_specs=pl.BlockSpec((rows, hps * B_V_DIM), lambda h, b, t: (b * steps + t, h)),
        out_shape=jax.ShapeDtypeStruct((n, B_WIDTH), BF16),
        compiler_params=_params("arbitrary", "arbitrary", "arbitrary"),
        name="diff_attention",
    )(qb, kb, vb, bias_t, lq1, lk1, lq2, lk2, subln)


def _load_token_major(ref):
    rows = ref.shape[0] // ROW_CHUNKS
    return jnp.concatenate([ref[pl.ds(c, rows, stride=ROW_CHUNKS), :] for c in range(ROW_CHUNKS)],
                           axis=-1)


def _store_token_major(ref, val, first_row=0):
    rows = val.shape[0]
    for c in range(ROW_CHUNKS):
        ref[pl.ds(first_row * ROW_CHUNKS + c, rows, stride=ROW_CHUNKS), :] = (
            val[:, c * LANES:(c + 1) * LANES])


def _route(biased):
    epg = EXPERTS_PER_GROUP
    group_score = []
    for g in range(N_GROUPS):
        v = biased[g * epg:(g + 1) * epg]
        best = None
        for a in range(epg):
            for b in range(a + 1, epg):
                pair = v[a] + v[b]
                best = pair if best is None else jnp.maximum(best, pair)
        group_score.append(best)
    sel = jnp.zeros_like(group_score[0], dtype=jnp.int32)
    top = group_score[0]
    for g in range(1, N_GROUPS):
        better = group_score[g] > top
        sel = jnp.where(better, g, sel)
        top = jnp.where(better, group_score[g], top)

    def pick(vals, k):
        out = vals[k]
        for g in range(1, N_GROUPS):
            out = jnp.where(sel == g, vals[g * epg + k], out)
        return out

    bsel = [pick(biased, k) for k in range(epg)]
    neg = jnp.full_like(bsel[0], -jnp.inf)

    def argmax_first(vals):
        idx = jnp.zeros_like(sel)
        top_v = vals[0]
        for k in range(1, epg):
            better = vals[k] > top_v
            idx = jnp.where(better, k, idx)
            top_v = jnp.where(better, vals[k], top_v)
        return idx

    i1 = argmax_first(bsel)
    i2 = argmax_first([jnp.where(i1 == k, neg, bsel[k]) for k in range(epg)])

    k_lo = jnp.minimum(i1, i2)
    k_hi = jnp.maximum(i1, i2)
    pair = jnp.where(k_lo == 0, k_hi - 1, jnp.where(k_lo == 1, k_hi + 1, PAIRS_PER_GROUP - 1))
    return sel * PAIRS_PER_GROUP + pair


def _merge_kernel(oa_ref, ob_ref, g_ref, x_ref, gate1_ref, shift_ref, scale_ref, nf_ref, wa_ref,
                  wb_ref, wo_ref, wr_ref, tri_ref, rb_ref, x2_ref, hx_ref, cls_ref, rank_ref,
                  count_ref, carry_ref):
    @pl.when(pl.program_id(0) == 0)
    def _():
        carry_ref[...] = jnp.zeros_like(carry_ref)

    rows = tri_ref.shape[0]
    carry = carry_ref[...]
    for sub in range(x_ref.shape[0] // rows):
        r = slice(sub * rows, (sub + 1) * rows)
        ya = jnp.dot(oa_ref[r, :], wa_ref[...], preferred_element_type=F32)
        yb = jnp.dot(ob_ref[r, :], wb_ref[...], preferred_element_type=F32)
        m = (g_ref[r, 0:D_MODEL].astype(F32) * ya
             + g_ref[r, D_MODEL:2 * D_MODEL].astype(F32) * yb)
        merged = jnp.dot(m.astype(BF16), wo_ref[...], preferred_element_type=F32)
        x2 = x_ref[r, :] + gate1_ref[0] * merged
        x2_ref[r, :] = x2
        h2 = _rms(x2) * nf_ref[...] * (1.0 + scale_ref[0]) + shift_ref[0]
        _store_token_major(hx_ref, h2, sub * rows)

        h_hi = h2.astype(BF16)
        h_lo = (h2 - h_hi.astype(F32)).astype(BF16)
        by_hi = jnp.dot(h_hi, wr_ref[...], preferred_element_type=F32)
        logits = (by_hi[:, 0:LANES] + by_hi[:, LANES:2 * LANES]
                  + jnp.dot(h_lo, wr_ref[:, 0:LANES], preferred_element_type=F32))
        scores_t = jnp.transpose(jax.nn.sigmoid(logits))
        biased = [scores_t[e:e + 1, :] + rb_ref[e] for e in range(N_EXPERTS)]
        cls = _route(biased)

        class_row = lax.broadcasted_iota(jnp.int32, (CLASS_ROWS, rows), 0)
        onehot = class_row == cls
        before = jnp.dot(jnp.where(onehot, 1.0, 0.0).astype(BF16), tri_ref[...],
                         preferred_element_type=F32)
        rank = jnp.sum(jnp.where(onehot, before + carry[:, 0:1], 0.0), axis=0, keepdims=True)
        carry = carry + jnp.sum(jnp.where(onehot, 1.0, 0.0), axis=1, keepdims=True)
        cls_ref[0, :, r] = cls
        rank_ref[0, :, r] = rank.astype(jnp.int32)
    carry_ref[...] = carry
    count_ref[...] = carry


def _merge(oa, ob, g, x, mod, norm_ffn, wa_b, wb_b, wo_b, wr_pad, tri, router_bias, seq):
    n = x.shape[0]
    tm = min(ROW_TILE, seq)
    tps = seq // tm
    nt = n // tm

    def row(width):
        return pl.BlockSpec((tm, width), lambda i: (i, 0))

    def full(a):
        return pl.BlockSpec(a.shape, lambda i: (0,) * a.ndim, pipeline_mode=pl.Buffered(1))

    def modspec(k):
        return pl.BlockSpec((1, 1, D_MODEL), lambda i: (N_MOD * (i // tps) + k, 0, 0))

    lane_vec = pl.BlockSpec((1, 1, tm), lambda i: (i, 0, 0))
    return pl.pallas_call(
        _merge_kernel,
        grid=(nt,),
        in_specs=[row(A_WIDTH), row(B_WIDTH), row(GATE_WIDTH), row(D_MODEL), modspec(2),
                  modspec(3), modspec(4), full(norm_ffn), full(wa_b), full(wb_b), full(wo_b),
                  full(wr_pad), full(tri), pl.BlockSpec(memory_space=pltpu.SMEM)],
        out_specs=[row(D_MODEL), pl.BlockSpec((tm * ROW_CHUNKS, LANES), lambda i: (i, 0)),
                   lane_vec, lane_vec, pl.BlockSpec((CLASS_ROWS, LANES), lambda i: (0, 0))],
        out_shape=[jax.ShapeDtypeStruct((n, D_MODEL), F32),
                   jax.ShapeDtypeStruct((n * ROW_CHUNKS, LANES), F32),
                   jax.ShapeDtypeStruct((nt, 1, tm), jnp.int32),
                   jax.ShapeDtypeStruct((nt, 1, tm), jnp.int32),
                   jax.ShapeDtypeStruct((CLASS_ROWS, LANES), F32)],
        scratch_shapes=[pltpu.VMEM((CLASS_ROWS, LANES), F32)],
        compiler_params=_params("arbitrary"),
        name="merge_route",
    )(oa, ob, g, x, mod, mod, mod, norm_ffn, wa_b, wb_b, wo_b, wr_pad, tri, router_bias)


def _row_copy(src_ref, src_row, dst_ref, dst_row, sem):
    return pltpu.make_async_copy(src_ref.at[src_row], dst_ref.at[dst_row], sem)


def _permute_rows(to_slot, slot_ref, tok_ref, slot_major_ref, sem):
    rows = tok_ref.shape[0]

    def body(j, carry):
        first = j * DMA_UNROLL
        slots = [slot_ref[first + k] for k in range(DMA_UNROLL)]
        for k in range(DMA_UNROLL):
            if to_slot:
                _row_copy(tok_ref, first + k, slot_major_ref, slots[k], sem).start(priority=k % 2)
            else:
                _row_copy(slot_major_ref, slots[k], tok_ref, first + k, sem).start(priority=k % 2)
        return carry

    lax.fori_loop(0, rows // DMA_UNROLL, body, 0)
    pltpu.make_async_copy(slot_major_ref.at[pl.ds(0, rows)], tok_ref, sem).wait()


def _dispatch_kernel(n_tiles, tile_rows, slot_ref, first_ref, tiles_ref, total_ref, src_ref,
                     dst_ref, zero_ref, sem, zero_sem):
    @pl.when(pl.program_id(0) == 0)
    def _():
        zero_ref[...] = jnp.zeros_like(zero_ref)

        def zero_tile(tile):
            return pltpu.make_async_copy(
                zero_ref, dst_ref.at[pl.ds(tile * tile_rows, tile_rows)], zero_sem)

        def start_tail(tile, carry):
            zero_tile(tile).start()
            return carry

        def wait_one(tile, carry):
            zero_tile(0).wait()
            return carry

        for c in range(N_CLASSES):
            @pl.when(tiles_ref[c] > 0)
            def _():
                zero_tile(first_ref[c] + tiles_ref[c] - 1).start()
        lax.fori_loop(total_ref[0], n_tiles, start_tail, 0)
        for c in range(N_CLASSES):
            @pl.when(tiles_ref[c] > 0)
            def _():
                zero_tile(0).wait()
        lax.fori_loop(total_ref[0], n_tiles, wait_one, 0)

    _permute_rows(True, slot_ref, src_ref, dst_ref, sem)


def _dispatch(hx, slot, plan):
    n = hx.shape[0]
    rows = min(DMA_ROWS, n)
    n_tiles = plan["n_tiles"]
    tile_rows = plan["tile_rows"]
    smem = pl.BlockSpec(memory_space=pltpu.SMEM)
    return pl.pallas_call(
        functools.partial(_dispatch_kernel, n_tiles, tile_rows),
        grid=(n // rows,),
        in_specs=[pl.BlockSpec((rows,), lambda i: (i,), memory_space=pltpu.SMEM), smem, smem, smem,
                  pl.BlockSpec((rows,) + hx.shape[1:], lambda i: (i, 0, 0))],
        out_specs=pl.BlockSpec(memory_space=pl.ANY),
        out_shape=jax.ShapeDtypeStruct((n_tiles * tile_rows,) + hx.shape[1:], hx.dtype),
        scratch_shapes=[pltpu.VMEM((tile_rows,) + hx.shape[1:], hx.dtype),
                        pltpu.SemaphoreType.DMA(()), pltpu.SemaphoreType.DMA(())],
        compiler_params=_params("arbitrary"),
        name="dispatch_rows",
    )(slot, plan["first_tile"], plan["tiles"], plan["total"], hx)


def _combine_kernel(slot_ref, src_ref, dst_ref, sem):
    _permute_rows(False, slot_ref, dst_ref, src_ref, sem)


def _combine(ys, slot, n):
    rows = min(DMA_ROWS, n)
    return pl.pallas_call(
        _combine_kernel,
        grid=(n // rows,),
        in_specs=[pl.BlockSpec((rows,), lambda i: (i,), memory_space=pltpu.SMEM),
                  pl.BlockSpec(memory_space=pl.ANY)],
        out_specs=pl.BlockSpec((rows,) + ys.shape[1:], lambda i: (i, 0, 0)),
        out_shape=jax.ShapeDtypeStruct((n,) + ys.shape[1:], ys.dtype),
        scratch_shapes=[pltpu.SemaphoreType.DMA(())],
        compiler_params=_params("arbitrary"),
        name="combine_rows",
    )(slot, ys)


def _moe_kernel(lo_ref, hi_ref, valid_ref, hs_ref, wr_ref, wg_lo, wu_lo, wd_lo, wg_hi, wu_hi,
                wd_hi, o_ref):
    t = pl.program_id(0)
    valid = valid_ref[t] > 0

    @pl.when(jnp.logical_not(valid))
    def _():
        o_ref[...] = jnp.zeros_like(o_ref)

    @pl.when(valid)
    def _():
        h32 = _load_token_major(hs_ref)
        h = h32.astype(BF16)

        def ffn(wg_ref, wu_ref, wd_ref):
            a = jnp.dot(h, wg_ref[0], preferred_element_type=F32)
            u = jnp.dot(h, wu_ref[0], preferred_element_type=F32)
            act = (a * jax.nn.sigmoid(a) * u).astype(BF16)
            return jnp.dot(act, wd_ref[0], preferred_element_type=F32)

        def score(e):
            return jax.nn.sigmoid(jnp.sum(h32 * wr_ref[pl.ds(e, 1), :], axis=-1, keepdims=True))

        s_lo = score(lax.rem(lo_ref[t], N_EXPERTS))
        s_hi = score(lax.rem(hi_ref[t], N_EXPERTS))
        denom = s_lo + s_hi
        y = ((s_lo / denom) * ffn(wg_lo, wu_lo, wd_lo)
             + (s_hi / denom) * ffn(wg_hi, wu_hi, wd_hi))
        _store_token_major(o_ref, y)


def _moe(hs, tile_rows, tile_lo, tile_hi, tile_valid, wr_t, wg_b, wu_b, wd_b):
    n_tiles = tile_lo.shape[0]
    up = (1, D_MODEL, D_FF_EXPERT)
    down = (1, D_FF_EXPERT, D_MODEL)

    def by(sel):
        return lambda t, lo, hi, valid: ((lo, hi)[sel][t], 0, 0)

    rows = pl.BlockSpec((tile_rows * ROW_CHUNKS, LANES), lambda t, lo, hi, valid: (t, 0))
    return pl.pallas_call(
        _moe_kernel,
        grid_spec=pltpu.PrefetchScalarGridSpec(
            num_scalar_prefetch=3,
            grid=(n_tiles,),
            in_specs=[rows, pl.BlockSpec(wr_t.shape, lambda t, lo, hi, valid: (0, 0)),
                      pl.BlockSpec(up, by(0)), pl.BlockSpec(up, by(0)), pl.BlockSpec(down, by(0)),
                      pl.BlockSpec(up, by(1)), pl.BlockSpec(up, by(1)), pl.BlockSpec(down, by(1))],
            out_specs=rows),
        out_shape=jax.ShapeDtypeStruct(hs.shape, F32),
        compiler_params=_params("arbitrary"),
        name="experts",
    )(tile_lo, tile_hi, tile_valid, hs, wr_t, wg_b, wu_b, wd_b, wg_b, wu_b, wd_b)


def _moe_tile_rows(n):
    return MOE_TILE_LARGE if n >= MOE_LARGE_MIN_TOKENS else MOE_TILE


def _tile_plan(counts, n):
    tile_rows = _moe_tile_rows(n)
    n_tiles = n // tile_rows + N_CLASSES
    tiles_c = (counts + tile_rows - 1) // tile_rows
    ends = jnp.cumsum(tiles_c)
    first_tile = ends - tiles_c
    total = ends[-1]
    t = jnp.arange(n_tiles, dtype=jnp.int32)
    last = jnp.maximum(total - 1, 0)
    tile_cls = jnp.sum((jnp.minimum(t, last)[:, None] >= ends[None, :]).astype(jnp.int32), axis=1)
    tile_cls = jnp.minimum(tile_cls, N_CLASSES - 1)
    k_lo, k_hi = np.triu_indices(EXPERTS_PER_GROUP, 1)
    group = np.arange(N_CLASSES) // PAIRS_PER_GROUP
    lo_table = jnp.asarray(group * EXPERTS_PER_GROUP + np.tile(k_lo, N_GROUPS), jnp.int32)
    hi_table = jnp.asarray(group * EXPERTS_PER_GROUP + np.tile(k_hi, N_GROUPS), jnp.int32)
    i32 = jnp.int32
    return {"first_tile": first_tile.astype(i32), "tiles": tiles_c.astype(i32),
            "total": total.astype(i32).reshape(1), "lo": lo_table[tile_cls],
            "hi": hi_table[tile_cls], "valid": (t < total).astype(i32), "n_tiles": n_tiles,
            "tile_rows": tile_rows}


def _final_kernel(x_ref, y_ref, gate2_ref, fn_ref, o_ref):
    o_ref[...] = _rms(x_ref[...] + gate2_ref[0] * _load_token_major(y_ref)) * fn_ref[...]


def _final(x2, y, mod, final_norm, seq):
    n = x2.shape[0]
    tm = min(ROW_TILE, seq)
    tps = seq // tm
    row = pl.BlockSpec((tm, D_MODEL), lambda i: (i, 0))
    return pl.pallas_call(
        _final_kernel,
        grid=(n // tm,),
        in_specs=[row, pl.BlockSpec((tm * ROW_CHUNKS, LANES), lambda i: (i, 0)),
                  pl.BlockSpec((1, 1, D_MODEL), lambda i: (N_MOD * (i // tps) + 5, 0, 0)),
                  pl.BlockSpec(final_norm.shape, lambda i: (0, 0))],
        out_specs=row,
        out_shape=jax.ShapeDtypeStruct((n, D_MODEL), F32),
        compiler_params=_params("arbitrary"),
        name="final_norm",
    )(x2, y, mod, final_norm)


def _rope_tables(seq):
    rows = seq // GRID_W
    row = jnp.repeat(jnp.arange(rows, dtype=F32), GRID_W)
    col = jnp.tile(jnp.arange(GRID_W, dtype=F32), rows)
    inv = ROPE_THETA ** (-jnp.arange(ROT_FREQS, dtype=F32) / ROT_FREQS)
    ang = jnp.stack([row[:, None] * inv, col[:, None] * inv], axis=1)
    cos = jnp.cos(ang)
    sin = jnp.sin(ang)
    cos_h = jnp.concatenate([cos, cos], axis=-1).reshape(seq, HEAD_DIM)
    sin_h = jnp.concatenate([-sin, sin], axis=-1).reshape(seq, HEAD_DIM)
    return jnp.tile(cos_h, (1, 2)), jnp.tile(sin_h, (1, 2))


def _block_ones():
    idx = np.arange(LANES) // HEAD_DIM
    return jnp.asarray((idx[:, None] == idx[None, :]).astype(np.float32), dtype=BF16)


def kernel(x_prompt, x_sample, c_prompt, c_sample, w_ada, b_ada, norm_attn, norm_ffn, w_in,
           q_norm_a, k_norm_a, lambda_q1, lambda_k1, lambda_q2, lambda_k2, subln_b, rel_bias,
           w_branch_a, w_branch_b, w_out, w_router, router_bias, w_gate, w_up, w_down,
           final_norm):
    depth = w_in.shape[0]
    w_in_b = w_in.astype(BF16)
    wa_b = w_branch_a.astype(BF16)
    wb_b = w_branch_b.astype(BF16)
    wo_b = w_out.astype(BF16)
    wg_b = w_gate.astype(BF16).reshape(-1, D_MODEL, D_FF_EXPERT)
    wu_b = w_up.astype(BF16).reshape(-1, D_MODEL, D_FF_EXPERT)
    wd_b = w_down.astype(BF16).reshape(-1, D_FF_EXPERT, D_MODEL)
    wr_hi = w_router.astype(BF16)
    wr_lo = (w_router - wr_hi.astype(F32)).astype(BF16)
    lane_pad = ((0, 0), (0, LANES - N_EXPERTS))
    wr_pad = jnp.concatenate([jnp.pad(wr_hi, lane_pad), jnp.pad(wr_lo, lane_pad)], axis=1)
    wr_t = w_router.T
    bd = _block_ones()
    fn = final_norm.reshape(1, -1)

    def run(x3, c):
        bsz, seq, _ = x3.shape
        tq = min(Q_TILE, seq)
        cos_t, sin_t = _rope_tables(seq)
        bias_t = _bias_table(rel_bias, seq, tq)
        n = bsz * seq
        x = x3.reshape(n, D_MODEL)
        sub_rows = min(ROW_TILE, seq) // MERGE_SUBTILES
        tri = jnp.asarray(np.triu(np.ones((sub_rows, sub_rows), np.float32), 1), BF16)
        delta = None
        prev_mod = None
        for l in range(depth):
            mod = _ada(c, w_ada, b_ada, l).reshape(bsz * N_MOD, 1, D_MODEL)
            outs = _inproj(
                x, delta, mod, prev_mod, norm_attn[l].reshape(1, -1), w_in_b[l],
                jnp.tile(q_norm_a[l], 2).reshape(1, -1), jnp.tile(k_norm_a[l], 2).reshape(1, -1),
                cos_t, sin_t, bd, seq)
            qa, ka, va, qb, kb, vb, g = outs[:7]
            if delta is not None:
                x = outs[7]
            oa = _gqa(qa, ka, va, bsz, seq)
            lam_init = 0.8 - 0.6 * math.exp(-0.3 * l)
            ob = _diff(qb, kb, vb, bias_t, lambda_q1[l].reshape(1, -1), lambda_k1[l].reshape(1, -1),
                       lambda_q2[l].reshape(1, -1), lambda_k2[l].reshape(1, -1),
                       subln_b[l].reshape(1, -1), lam_init, bsz, seq)
            x, hx, cls, rank, counts = _merge(oa, ob, g, x, mod, norm_ffn[l].reshape(1, -1),
                                              wa_b[l], wb_b[l], wo_b[l], wr_pad, tri, router_bias,
                                              seq)
            cls = cls.reshape(n)
            rank = rank.reshape(n)
            plan = _tile_plan(counts[:N_CLASSES, 0].astype(jnp.int32), n)
            n_slots = plan["n_tiles"] * plan["tile_rows"]
            slot = (plan["first_tile"] * plan["tile_rows"])[cls] + rank
            hs = _dispatch(hx.reshape(n, ROW_CHUNKS, LANES), slot, plan)
            ys = _moe(hs.reshape(n_slots * ROW_CHUNKS, LANES), plan["tile_rows"],
                      plan["lo"] + l * N_EXPERTS, plan["hi"] + l * N_EXPERTS, plan["valid"], wr_t,
                      wg_b, wu_b, wd_b)
            delta = _combine(ys.reshape(n_slots, ROW_CHUNKS, LANES), slot, n)
            delta = delta.reshape(n * ROW_CHUNKS, LANES)
            prev_mod = mod
        return _final(x, delta, prev_mod, fn, seq).reshape(bsz, seq, D_MODEL)

    return (run(x_prompt, c_prompt), run(x_sample, c_sample))
```

```python
import functools
import math

import numpy as np
import jax
import jax.numpy as jnp
from jax import lax
from jax.experimental import pallas as pl
from jax.experimental.pallas import tpu as pltpu

F32 = jnp.float32
BF16 = jnp.bfloat16

D_MODEL = 1024
GRID_W = 64
HEAD_DIM = 64
EPS = 1e-6
A_HEADS = 8
A_KV_HEADS = 2
A_WIDTH = A_HEADS * HEAD_DIM
A_KV_WIDTH = A_KV_HEADS * HEAD_DIM
ROT_FREQS = HEAD_DIM // 4
ROPE_THETA = 10000.0
B_HEADS = 4
B_V_DIM = 2 * HEAD_DIM
B_QK_WIDTH = B_HEADS * 2 * HEAD_DIM
B_WIDTH = B_HEADS * B_V_DIM
GATE_WIDTH = 2 * D_MODEL
REL_BUCKETS = 32
REL_MAX_DIST = 128
N_EXPERTS = 16
N_GROUPS = 4
EXPERTS_PER_GROUP = N_EXPERTS // N_GROUPS
D_FF_EXPERT = 512
N_MOD = 6
PAIRS_PER_GROUP = EXPERTS_PER_GROUP * (EXPERTS_PER_GROUP - 1) // 2
N_CLASSES = N_GROUPS * PAIRS_PER_GROUP
CLASS_ROWS = 32
ROW_CHUNKS = D_MODEL // 128

OFF_QA = 0
OFF_KA = OFF_QA + A_WIDTH
OFF_VA = OFF_KA + A_KV_WIDTH
OFF_QB = OFF_VA + A_KV_WIDTH
OFF_KB = OFF_QB + B_QK_WIDTH
OFF_VB = OFF_KB + B_QK_WIDTH
OFF_G = OFF_VB + B_WIDTH
IN_COLS = OFF_G + GATE_WIDTH

LOG2E = math.log2(math.e)
QK_SCALE = HEAD_DIM ** -0.5 * LOG2E

LANES = 128
VMEM_LIMIT = 56 * 1024 * 1024

ROW_TILE = 512
MERGE_SUBTILES = 1
Q_TILE = 256
GQA_Q_TILE = 512
GQA_KV_PER_STEP = 1
GQA_TILES_PER_STEP = 4
DIFF_HEADS_PER_STEP = 4
DIFF_TILES_PER_STEP = 2
BIAS_CHUNK = 256
MOE_TILE = 256
MOE_TILE_LARGE = 512
MOE_LARGE_MIN_TOKENS = 32768
DMA_ROWS = 2048
DMA_UNROLL = 8


def _params(*sem):
    return pltpu.CompilerParams(dimension_semantics=sem, vmem_limit_bytes=VMEM_LIMIT)


def _rms(x, eps=EPS):
    return x * lax.rsqrt(jnp.mean(x * x, axis=-1, keepdims=True) + eps)


def _lane_ids(shape):
    return lax.broadcasted_iota(jnp.int32, shape, len(shape) - 1)


def _ada_kernel(c_ref, w_ref, b_ref, o_ref):
    c = c_ref[...]
    a = (c * jax.nn.sigmoid(c)).astype(BF16)
    o_ref[...] = jnp.dot(a, w_ref[0].astype(BF16), preferred_element_type=F32) + b_ref[0]


def _ada(c, w_ada, b_ada, layer):
    bsz = c.shape[0]
    ncol = w_ada.shape[2] // D_MODEL
    return pl.pallas_call(
        _ada_kernel,
        grid=(ncol,),
        in_specs=[pl.BlockSpec((bsz, D_MODEL), lambda j: (0, 0)),
                  pl.BlockSpec((1, D_MODEL, D_MODEL), lambda j: (layer, 0, j)),
                  pl.BlockSpec((1, 1, D_MODEL), lambda j: (layer, 0, j))],
        out_specs=pl.BlockSpec((bsz, D_MODEL), lambda j: (0, j)),
        out_shape=jax.ShapeDtypeStruct((bsz, w_ada.shape[2]), F32),
        compiler_params=_params("arbitrary"),
        name="ada_mod",
    )(c, w_ada, b_ada.reshape(b_ada.shape[0], 1, -1))


def _bias_kernel(rb_ref, bucket_ref, o_ref):
    bucket = bucket_ref[...]
    accs = [jnp.zeros(bucket.shape, F32) for _ in range(B_HEADS)]
    for b in range(REL_BUCKETS):
        hit = bucket == b
        for h in range(B_HEADS):
            accs[h] = jnp.where(hit, rb_ref[b * B_HEADS + h], accs[h])
    for h in range(B_HEADS):
        o_ref[h] = accs[h] * LOG2E


def _rel_bucket(rel):
    nb = REL_BUCKETS // 2
    max_exact = nb // 2
    ret = jnp.where(rel > 0, nb, 0)
    n = jnp.abs(rel)
    large = max_exact + (jnp.log(jnp.maximum(n, 1).astype(F32) / max_exact)
                         / math.log(REL_MAX_DIST / max_exact) * (nb - max_exact)).astype(jnp.int32)
    large = jnp.minimum(large, nb - 1)
    return ret + jnp.where(n < max_exact, n, large)


def _bias_table(rel_bias, seq, tq):
    nq = seq // tq
    width = seq + (nq - 1) * tq
    r = jnp.arange(tq, dtype=jnp.int32)[:, None]
    c = jnp.arange(width, dtype=jnp.int32)[None, :]
    bucket = _rel_bucket(c - (nq - 1) * tq - r).astype(jnp.int32)
    return pl.pallas_call(
        _bias_kernel,
        grid=(width // BIAS_CHUNK,),
        in_specs=[pl.BlockSpec(memory_space=pltpu.SMEM),
                  pl.BlockSpec((tq, BIAS_CHUNK), lambda j: (0, j))],
        out_specs=pl.BlockSpec((B_HEADS, tq, BIAS_CHUNK), lambda j: (0, 0, j)),
        out_shape=jax.ShapeDtypeStruct((B_HEADS, tq, width), F32),
        compiler_params=_params("arbitrary"),
        name="rel_bias_table",
    )(rel_bias.reshape(-1), bucket)


def _head_norm_rope(y, gain, cos, sin):
    lane = _lane_ids(y.shape)
    low = lane < HEAD_DIM
    sq = y * y
    ss = jnp.where(low, jnp.sum(jnp.where(low, sq, 0.0), axis=-1, keepdims=True),
                   jnp.sum(jnp.where(low, 0.0, sq), axis=-1, keepdims=True))
    yn = y * lax.rsqrt(ss * (1.0 / HEAD_DIM) + EPS) * gain
    partner = jnp.where((lane & ROT_FREQS) == 0,
                        pltpu.roll(yn, LANES - ROT_FREQS, 1),
                        pltpu.roll(yn, ROT_FREQS, 1))
    return yn * cos + partner * sin


def _inproj_kernel(has_delta, *refs):
    if has_delta:
        x_ref, y_ref, gate2_ref = refs[:3]
        refs = refs[3:]
        x = x_ref[...] + gate2_ref[0] * _load_token_major(y_ref)
        xo_ref = refs[-1]
        xo_ref[...] = x
        refs = refs[:-1]
    else:
        x_ref = refs[0]
        refs = refs[1:]
        x = x_ref[...]
    (shift_ref, scale_ref, na_ref, w_ref, qn_ref, kn_ref, cos_ref, sin_ref,
     qa_ref, ka_ref, va_ref, qb_ref, kb_ref, vb_ref, g_ref) = refs
    h = _rms(x) * na_ref[...] * (1.0 + scale_ref[0]) + shift_ref[0]
    hb = h.astype(BF16)

    def seg(lo, width):
        return jnp.dot(hb, w_ref[:, lo:lo + width], preferred_element_type=F32)

    cos = cos_ref[...]
    sin = sin_ref[...]

    qa = seg(OFF_QA, A_WIDTH)
    for j in range(A_WIDTH // LANES):
        y = _head_norm_rope(qa[:, j * LANES:(j + 1) * LANES], qn_ref[...], cos, sin)
        qa_ref[:, j * LANES:(j + 1) * LANES] = (y * QK_SCALE).astype(BF16)

    lane = _lane_ids((x.shape[0], LANES))
    low = lane < HEAD_DIM
    kva = seg(OFF_KA, 2 * A_KV_WIDTH)
    ka = _head_norm_rope(kva[:, 0:LANES], kn_ref[...], cos, sin)
    ka_sw = pltpu.roll(ka, HEAD_DIM, 1)
    ka_ref[:, 0:LANES] = jnp.where(low, ka, ka_sw).astype(BF16)
    ka_ref[:, LANES:2 * LANES] = jnp.where(low, ka_sw, ka).astype(BF16)
    va = kva[:, LANES:2 * LANES]
    va_sw = pltpu.roll(va, HEAD_DIM, 1)
    va_ref[:, 0:LANES] = jnp.where(low, va, 1.0).astype(BF16)
    va_ref[:, LANES:2 * LANES] = jnp.where(low, 1.0, va_sw).astype(BF16)
    va_ref[:, 2 * LANES:3 * LANES] = jnp.where(low, va_sw, 1.0).astype(BF16)
    va_ref[:, 3 * LANES:4 * LANES] = jnp.where(low, 1.0, va).astype(BF16)

    qb_ref[...] = (seg(OFF_QB, B_QK_WIDTH) * QK_SCALE).astype(BF16)
    kb_ref[...] = seg(OFF_KB, B_QK_WIDTH).astype(BF16)
    ones = jnp.ones((x.shape[0], B_V_DIM), BF16)
    vb = seg(OFF_VB, B_WIDTH).astype(BF16)
    for hd in range(B_HEADS):
        vb_ref[:, 2 * hd * B_V_DIM:(2 * hd + 1) * B_V_DIM] = vb[:, hd * B_V_DIM:(hd + 1) * B_V_DIM]
        vb_ref[:, (2 * hd + 1) * B_V_DIM:(2 * hd + 2) * B_V_DIM] = ones
    chunk = 512
    for j in range(GATE_WIDTH // chunk):
        g_ref[:, j * chunk:(j + 1) * chunk] = jax.nn.sigmoid(seg(OFF_G + j * chunk, chunk)).astype(BF16)


def _inproj(x, delta, mod, prev_mod, norm_attn, w_in_b, qn, kn, cos_t, sin_t, seq):
    n = x.shape[0]
    tm = min(ROW_TILE, seq)
    tps = seq // tm
    has_delta = delta is not None

    def row(width):
        return pl.BlockSpec((tm, width), lambda i: (i, 0))

    def full(a):
        return pl.BlockSpec(a.shape, lambda i: (0,) * a.ndim, pipeline_mode=pl.Buffered(1))

    def modspec(k):
        return pl.BlockSpec((1, 1, D_MODEL), lambda i: (N_MOD * (i // tps) + k, 0, 0))

    def row_out(width, dtype):
        return row(width), jax.ShapeDtypeStruct((n, width), dtype)

    outs = [row_out(A_WIDTH, BF16), row_out(2 * LANES, BF16), row_out(4 * LANES, BF16),
            row_out(B_QK_WIDTH, BF16), row_out(B_QK_WIDTH, BF16), row_out(2 * B_WIDTH, BF16),
            row_out(GATE_WIDTH, BF16)]
    args = [x]
    in_specs = [row(D_MODEL)]
    if has_delta:
        args += [delta, prev_mod]
        in_specs += [pl.BlockSpec((tm * ROW_CHUNKS, LANES), lambda i: (i, 0)), modspec(5)]
        outs = outs + [row_out(D_MODEL, F32)]
    args += [mod, mod, norm_attn, w_in_b, qn, kn, cos_t, sin_t]
    in_specs += [modspec(0), modspec(1), full(norm_attn), full(w_in_b), full(qn), full(kn),
                 pl.BlockSpec((tm, LANES), lambda i: (i % tps, 0)),
                 pl.BlockSpec((tm, LANES), lambda i: (i % tps, 0))]
    return pl.pallas_call(
        functools.partial(_inproj_kernel, has_delta),
        grid=(n // tm,),
        in_specs=in_specs,
        out_specs=[spec for spec, _ in outs],
        out_shape=[shape for _, shape in outs],
        compiler_params=_params("arbitrary"),
        name="in_proj",
    )(*args)


_NT = (((1,), (1,)), ((), ()))


def _exp_scores(s):
    return jnp.exp2(s - jnp.max(s, axis=-1, keepdims=True)).astype(BF16)


def _gqa_kernel(tq, q_ref, k_ref, v_ref, o_ref):
    low = _lane_ids((tq, LANES)) < HEAD_DIM
    for tile in range(q_ref.shape[0] // tq):
        rows = slice(tile * tq, (tile + 1) * tq)
        for pair in range(q_ref.shape[1] // LANES):
            kv = pair // (A_HEADS // A_KV_HEADS // 2)
            k = k_ref[:, kv * LANES:(kv + 1) * LANES]
            q = q_ref[rows, pair * LANES:(pair + 1) * LANES]
            zero = jnp.zeros_like(q)
            normed = []
            for half, qh in enumerate((jnp.where(low, q, zero), jnp.where(low, zero, q))):
                e = _exp_scores(lax.dot_general(qh, k, _NT, preferred_element_type=F32))
                v = v_ref[:, (2 * kv + half) * LANES:(2 * kv + half + 1) * LANES]
                ol = jnp.dot(e, v, preferred_element_type=F32)
                normed.append(ol / pltpu.roll(ol, HEAD_DIM, 1))
            o_ref[rows, pair * LANES:(pair + 1) * LANES] = jnp.where(
                low, normed[0], normed[1]).astype(o_ref.dtype)


def _gqa(qa, ka, va, bsz, seq):
    n = qa.shape[0]
    tq = min(GQA_Q_TILE * GQA_TILES_PER_STEP, seq)
    nq = seq // tq
    kvps = GQA_KV_PER_STEP
    group_width = kvps * A_WIDTH // A_KV_HEADS
    return pl.pallas_call(
        functools.partial(_gqa_kernel, min(GQA_Q_TILE, seq)),
        grid=(bsz, A_KV_HEADS // kvps, nq),
        in_specs=[pl.BlockSpec((tq, group_width), lambda b, j, t: (b * nq + t, j)),
                  pl.BlockSpec((seq, kvps * LANES), lambda b, j, t: (b, j)),
                  pl.BlockSpec((seq, kvps * 2 * LANES), lambda b, j, t: (b, j))],
        out_specs=pl.BlockSpec((tq, group_width), lambda b, j, t: (b * nq + t, j)),
        out_shape=jax.ShapeDtypeStruct((n, A_WIDTH), BF16),
        compiler_params=_params("arbitrary", "arbitrary", "arbitrary"),
        name="gqa_attention",
    )(qa, ka, va)


def _diff_kernel(lam_init, nq, seq, q_ref, k_ref, v_ref, bias_ref, lq1_ref, lk1_ref, lq2_ref,
                 lk2_ref, sub_ref, o_ref):
    tq = bias_ref.shape[1]
    tiles = q_ref.shape[0] // tq
    lam = (jnp.exp(jnp.sum(lq1_ref[...] * lk1_ref[...], axis=-1, keepdims=True))
           - jnp.exp(jnp.sum(lq2_ref[...] * lk2_ref[...], axis=-1, keepdims=True)) + lam_init)
    low = _lane_ids((tq, LANES)) < HEAD_DIM
    for tile in range(tiles):
        rows = slice(tile * tq, (tile + 1) * tq)
        t = pl.program_id(2) * tiles + tile
        start = pl.multiple_of((nq - 1 - t) * tq, LANES)
        for hd in range(DIFF_HEADS_PER_STEP):
            q = q_ref[rows, hd * LANES:(hd + 1) * LANES]
            k = k_ref[:, hd * LANES:(hd + 1) * LANES]
            v = v_ref[:, 2 * hd * B_V_DIM:(2 * hd + 2) * B_V_DIM]
            zero = jnp.zeros_like(q)
            bias = bias_ref[hd, :, pl.ds(start, seq)]
            parts = []
            for qh in (jnp.where(low, q, zero), jnp.where(low, zero, q)):
                e = _exp_scores(lax.dot_general(qh, k, _NT, preferred_element_type=F32) + bias)
                ol = jnp.dot(e, v, preferred_element_type=F32)
                parts.append(ol[:, 0:B_V_DIM] / ol[:, B_V_DIM:2 * B_V_DIM])
            o = parts[0] - lam * parts[1]
            o_ref[rows, hd * B_V_DIM:(hd + 1) * B_V_DIM] = (
                _rms(o) * sub_ref[...] * (1.0 - lam_init)).astype(o_ref.dtype)


def _diff(qb, kb, vb, bias_t, lq1, lk1, lq2, lk2, subln, lam_init, bsz, seq):
    n = qb.shape[0]
    tq = bias_t.shape[1]
    nq = seq // tq
    hps = DIFF_HEADS_PER_STEP
    rows = tq * min(DIFF_TILES_PER_STEP, nq)
    steps = seq // rows

    def small(a):
        return pl.BlockSpec(a.shape, lambda h, b, t: (0, 0))

    return pl.pallas_call(
        functools.partial(_diff_kernel, lam_init, nq, seq),
        grid=(B_HEADS // hps, bsz, steps),
        in_specs=[pl.BlockSpec((rows, hps * LANES), lambda h, b, t: (b * steps + t, h)),
                  pl.BlockSpec((seq, hps * LANES), lambda h, b, t: (b, h)),
                  pl.BlockSpec((seq, hps * 2 * B_V_DIM), lambda h, b, t: (b, h)),
                  pl.BlockSpec((hps, tq, bias_t.shape[2]), lambda h, b, t: (h, 0, 0),
                               pipeline_mode=pl.Buffered(1)),
                  small(lq1), small(lk1), small(lq2), small(lk2), small(subln)],
        out_specs=pl.BlockSpec((rows, hps * B_V_DIM), lambda h, b, t: (b * steps + t, h)),
        out_shape=jax.ShapeDtypeStruct((n, B_WIDTH), BF16),
        compiler_params=_params("arbitrary", "arbitrary", "arbitrary"),
        name="diff_attention",
    )(qb, kb, vb, bias_t, lq1, lk1, lq2, lk2, subln)


def _load_token_major(ref):
    rows = ref.shape[0] // ROW_CHUNKS
    return jnp.concatenate([ref[pl.ds(c, rows, stride=ROW_CHUNKS), :] for c in range(ROW_CHUNKS)],
                           axis=-1)


def _store_token_major(ref, val, first_row=0):
    rows = val.shape[0]
    for c in range(ROW_CHUNKS):
        ref[pl.ds(first_row * ROW_CHUNKS + c, rows, stride=ROW_CHUNKS), :] = (
            val[:, c * LANES:(c + 1) * LANES])


def _route(biased):
    epg = EXPERTS_PER_GROUP
    group_score = []
    for g in range(N_GROUPS):
        v = biased[g * epg:(g + 1) * epg]
        best = None
        for a in range(epg):
            for b in range(a + 1, epg):
                pair = v[a] + v[b]
                best = pair if best is None else jnp.maximum(best, pair)
        group_score.append(best)
    sel = jnp.zeros_like(group_score[0], dtype=jnp.int32)
    top = group_score[0]
    for g in range(1, N_GROUPS):
        better = group_score[g] > top
        sel = jnp.where(better, g, sel)
        top = jnp.where(better, group_score[g], top)

    def pick(vals, k):
        out = vals[k]
        for g in range(1, N_GROUPS):
            out = jnp.where(sel == g, vals[g * epg + k], out)
        return out

    bsel = [pick(biased, k) for k in range(epg)]
    neg = jnp.full_like(bsel[0], -jnp.inf)

    def argmax_first(vals):
        idx = jnp.zeros_like(sel)
        top_v = vals[0]
        for k in range(1, epg):
            better = vals[k] > top_v
            idx = jnp.where(better, k, idx)
            top_v = jnp.where(better, vals[k], top_v)
        return idx

    i1 = argmax_first(bsel)
    i2 = argmax_first([jnp.where(i1 == k, neg, bsel[k]) for k in range(epg)])

    k_lo = jnp.minimum(i1, i2)
    k_hi = jnp.maximum(i1, i2)
    pair = jnp.where(k_lo == 0, k_hi - 1, jnp.where(k_lo == 1, k_hi + 1, PAIRS_PER_GROUP - 1))
    return sel * PAIRS_PER_GROUP + pair


def _merge_kernel(oa_ref, ob_ref, g_ref, x_ref, gate1_ref, shift_ref, scale_ref, nf_ref, wa_ref,
                  wb_ref, wo_ref, wr_ref, tri_ref, rb_ref, x2_ref, hx_ref, cls_ref, rank_ref,
                  count_ref, carry_ref):
    @pl.when(pl.program_id(0) == 0)
    def _():
        carry_ref[...] = jnp.zeros_like(carry_ref)

    rows = tri_ref.shape[0]
    carry = carry_ref[...]
    for sub in range(x_ref.shape[0] // rows):
        r = slice(sub * rows, (sub + 1) * rows)
        ya = jnp.dot(oa_ref[r, :], wa_ref[...], preferred_element_type=F32)
        yb = jnp.dot(ob_ref[r, :], wb_ref[...], preferred_element_type=F32)
        m = (g_ref[r, 0:D_MODEL].astype(F32) * ya
             + g_ref[r, D_MODEL:2 * D_MODEL].astype(F32) * yb)
        merged = jnp.dot(m.astype(BF16), wo_ref[...], preferred_element_type=F32)
        x2 = x_ref[r, :] + gate1_ref[0] * merged
        x2_ref[r, :] = x2
        h2 = _rms(x2) * nf_ref[...] * (1.0 + scale_ref[0]) + shift_ref[0]
        _store_token_major(hx_ref, h2, sub * rows)

        h_hi = h2.astype(BF16)
        h_lo = (h2 - h_hi.astype(F32)).astype(BF16)
        by_hi = jnp.dot(h_hi, wr_ref[...], preferred_element_type=F32)
        logits = (by_hi[:, 0:LANES] + by_hi[:, LANES:2 * LANES]
                  + jnp.dot(h_lo, wr_ref[:, 0:LANES], preferred_element_type=F32))
        scores_t = jnp.transpose(jax.nn.sigmoid(logits))
        biased = [scores_t[e:e + 1, :] + rb_ref[e] for e in range(N_EXPERTS)]
        cls = _route(biased)

        class_row = lax.broadcasted_iota(jnp.int32, (CLASS_ROWS, rows), 0)
        onehot = class_row == cls
        before = jnp.dot(jnp.where(onehot, 1.0, 0.0).astype(BF16), tri_ref[...],
                         preferred_element_type=F32)
        rank = jnp.sum(jnp.where(onehot, before + carry[:, 0:1], 0.0), axis=0, keepdims=True)
        carry = carry + jnp.sum(jnp.where(onehot, 1.0, 0.0), axis=1, keepdims=True)
        cls_ref[0, :, r] = cls
        rank_ref[0, :, r] = rank.astype(jnp.int32)
    carry_ref[...] = carry
    count_ref[...] = carry


def _merge(oa, ob, g, x, mod, norm_ffn, wa_b, wb_b, wo_b, wr_pad, tri, router_bias, seq):
    n = x.shape[0]
    tm = min(ROW_TILE, seq)
    tps = seq // tm
    nt = n // tm

    def row(width):
        return pl.BlockSpec((tm, width), lambda i: (i, 0))

    def full(a):
        return pl.BlockSpec(a.shape, lambda i: (0,) * a.ndim, pipeline_mode=pl.Buffered(1))

    def modspec(k):
        return pl.BlockSpec((1, 1, D_MODEL), lambda i: (N_MOD * (i // tps) + k, 0, 0))

    lane_vec = pl.BlockSpec((1, 1, tm), lambda i: (i, 0, 0))
    return pl.pallas_call(
        _merge_kernel,
        grid=(nt,),
        in_specs=[row(A_WIDTH), row(B_WIDTH), row(GATE_WIDTH), row(D_MODEL), modspec(2),
                  modspec(3), modspec(4), full(norm_ffn), full(wa_b), full(wb_b), full(wo_b),
                  full(wr_pad), full(tri), pl.BlockSpec(memory_space=pltpu.SMEM)],
        out_specs=[row(D_MODEL), pl.BlockSpec((tm * ROW_CHUNKS, LANES), lambda i: (i, 0)),
                   lane_vec, lane_vec, pl.BlockSpec((CLASS_ROWS, LANES), lambda i: (0, 0))],
        out_shape=[jax.ShapeDtypeStruct((n, D_MODEL), F32),
                   jax.ShapeDtypeStruct((n * ROW_CHUNKS, LANES), F32),
                   jax.ShapeDtypeStruct((nt, 1, tm), jnp.int32),
                   jax.ShapeDtypeStruct((nt, 1, tm), jnp.int32),
                   jax.ShapeDtypeStruct((CLASS_ROWS, LANES), F32)],
        scratch_shapes=[pltpu.VMEM((CLASS_ROWS, LANES), F32)],
        compiler_params=_params("arbitrary"),
        name="merge_route",
    )(oa, ob, g, x, mod, mod, mod, norm_ffn, wa_b, wb_b, wo_b, wr_pad, tri, router_bias)


def _row_copy(src_ref, src_row, dst_ref, dst_row, sem):
    return pltpu.make_async_copy(src_ref.at[src_row], dst_ref.at[dst_row], sem)


def _permute_rows(to_slot, slot_ref, tok_ref, slot_major_ref, sem):
    rows = tok_ref.shape[0]

    def body(j, carry):
        first = j * DMA_UNROLL
        slots = [slot_ref[first + k] for k in range(DMA_UNROLL)]
        for k in range(DMA_UNROLL):
            if to_slot:
                _row_copy(tok_ref, first + k, slot_major_ref, slots[k], sem).start(priority=k % 2)
            else:
                _row_copy(slot_major_ref, slots[k], tok_ref, first + k, sem).start(priority=k % 2)
        return carry

    lax.fori_loop(0, rows // DMA_UNROLL, body, 0)
    pltpu.make_async_copy(slot_major_ref.at[pl.ds(0, rows)], tok_ref, sem).wait()


def _dispatch_kernel(n_tiles, tile_rows, slot_ref, first_ref, tiles_ref, total_ref, src_ref,
                     dst_ref, zero_ref, sem, zero_sem):
    @pl.when(pl.program_id(0) == 0)
    def _():
        zero_ref[...] = jnp.zeros_like(zero_ref)

        def zero_tile(tile):
            return pltpu.make_async_copy(
                zero_ref, dst_ref.at[pl.ds(tile * tile_rows, tile_rows)], zero_sem)

        def start_tail(tile, carry):
            zero_tile(tile).start()
            return carry

        def wait_one(tile, carry):
            zero_tile(0).wait()
            return carry

        for c in range(N_CLASSES):
            @pl.when(tiles_ref[c] > 0)
            def _():
                zero_tile(first_ref[c] + tiles_ref[c] - 1).start()
        lax.fori_loop(total_ref[0], n_tiles, start_tail, 0)
        for c in range(N_CLASSES):
            @pl.when(tiles_ref[c] > 0)
            def _():
                zero_tile(0).wait()
        lax.fori_loop(total_ref[0], n_tiles, wait_one, 0)

    _permute_rows(True, slot_ref, src_ref, dst_ref, sem)


def _dispatch(hx, slot, plan):
    n = hx.shape[0]
    rows = min(DMA_ROWS, n)
    n_tiles = plan["n_tiles"]
    tile_rows = plan["tile_rows"]
    smem = pl.BlockSpec(memory_space=pltpu.SMEM)
    return pl.pallas_call(
        functools.partial(_dispatch_kernel, n_tiles, tile_rows),
        grid=(n // rows,),
        in_specs=[pl.BlockSpec((rows,), lambda i: (i,), memory_space=pltpu.SMEM), smem, smem, smem,
                  pl.BlockSpec((rows,) + hx.shape[1:], lambda i: (i, 0, 0))],
        out_specs=pl.BlockSpec(memory_space=pl.ANY),
        out_shape=jax.ShapeDtypeStruct((n_tiles * tile_rows,) + hx.shape[1:], hx.dtype),
        scratch_shapes=[pltpu.VMEM((tile_rows,) + hx.shape[1:], hx.dtype),
                        pltpu.SemaphoreType.DMA(()), pltpu.SemaphoreType.DMA(())],
        compiler_params=_params("arbitrary"),
        name="dispatch_rows",
    )(slot, plan["first_tile"], plan["tiles"], plan["total"], hx)


def _combine_kernel(slot_ref, src_ref, dst_ref, sem):
    _permute_rows(False, slot_ref, dst_ref, src_ref, sem)


def _combine(ys, slot, n):
    rows = min(DMA_ROWS, n)
    return pl.pallas_call(
        _combine_kernel,
        grid=(n // rows,),
        in_specs=[pl.BlockSpec((rows,), lambda i: (i,), memory_space=pltpu.SMEM),
                  pl.BlockSpec(memory_space=pl.ANY)],
        out_specs=pl.BlockSpec((rows,) + ys.shape[1:], lambda i: (i, 0, 0)),
        out_shape=jax.ShapeDtypeStruct((n,) + ys.shape[1:], ys.dtype),
        scratch_shapes=[pltpu.SemaphoreType.DMA(())],
        compiler_params=_params("arbitrary"),
        name="combine_rows",
    )(slot, ys)


def _moe_kernel(lo_ref, hi_ref, valid_ref, hs_ref, wr_ref, wg_lo, wu_lo, wd_lo, wg_hi, wu_hi,
                wd_hi, o_ref):
    t = pl.program_id(0)
    valid = valid_ref[t] > 0

    @pl.when(jnp.logical_not(valid))
    def _():
        o_ref[...] = jnp.zeros_like(o_ref)

    @pl.when(valid)
    def _():
        h32 = _load_token_major(hs_ref)
        h = h32.astype(BF16)

        def ffn(wg_ref, wu_ref, wd_ref):
            a = jnp.dot(h, wg_ref[0], preferred_element_type=F32)
            u = jnp.dot(h, wu_ref[0], preferred_element_type=F32)
            act = (a * jax.nn.sigmoid(a) * u).astype(BF16)
            return jnp.dot(act, wd_ref[0], preferred_element_type=F32)

        def score(e):
            return jax.nn.sigmoid(jnp.sum(h32 * wr_ref[pl.ds(e, 1), :], axis=-1, keepdims=True))

        s_lo = score(lax.rem(lo_ref[t], N_EXPERTS))
        s_hi = score(lax.rem(hi_ref[t], N_EXPERTS))
        denom = s_lo + s_hi
        y = ((s_lo / denom) * ffn(wg_lo, wu_lo, wd_lo)
             + (s_hi / denom) * ffn(wg_hi, wu_hi, wd_hi))
        _store_token_major(o_ref, y)


def _moe(hs, tile_rows, tile_lo, tile_hi, tile_valid, wr_t, wg_b, wu_b, wd_b):
    n_tiles = tile_lo.shape[0]
    up = (1, D_MODEL, D_FF_EXPERT)
    down = (1, D_FF_EXPERT, D_MODEL)

    def by(sel):
        return lambda t, lo, hi, valid: ((lo, hi)[sel][t], 0, 0)

    rows = pl.BlockSpec((tile_rows * ROW_CHUNKS, LANES), lambda t, lo, hi, valid: (t, 0))
    return pl.pallas_call(
        _moe_kernel,
        grid_spec=pltpu.PrefetchScalarGridSpec(
            num_scalar_prefetch=3,
            grid=(n_tiles,),
            in_specs=[rows, pl.BlockSpec(wr_t.shape, lambda t, lo, hi, valid: (0, 0)),
                      pl.BlockSpec(up, by(0)), pl.BlockSpec(up, by(0)), pl.BlockSpec(down, by(0)),
                      pl.BlockSpec(up, by(1)), pl.BlockSpec(up, by(1)), pl.BlockSpec(down, by(1))],
            out_specs=rows),
        out_shape=jax.ShapeDtypeStruct(hs.shape, F32),
        compiler_params=_params("arbitrary"),
        name="experts",
    )(tile_lo, tile_hi, tile_valid, hs, wr_t, wg_b, wu_b, wd_b, wg_b, wu_b, wd_b)


def _moe_tile_rows(n):
    return MOE_TILE_LARGE if n >= MOE_LARGE_MIN_TOKENS else MOE_TILE


def _tile_plan(counts, n):
    tile_rows = _moe_tile_rows(n)
    n_tiles = n // tile_rows + N_CLASSES
    tiles_c = (counts + tile_rows - 1) // tile_rows
    ends = jnp.cumsum(tiles_c)
    first_tile = ends - tiles_c
    total = ends[-1]
    t = jnp.arange(n_tiles, dtype=jnp.int32)
    last = jnp.maximum(total - 1, 0)
    tile_cls = jnp.sum((jnp.minimum(t, last)[:, None] >= ends[None, :]).astype(jnp.int32), axis=1)
    tile_cls = jnp.minimum(tile_cls, N_CLASSES - 1)
    k_lo, k_hi = np.triu_indices(EXPERTS_PER_GROUP, 1)
    group = np.arange(N_CLASSES) // PAIRS_PER_GROUP
    lo_table = jnp.asarray(group * EXPERTS_PER_GROUP + np.tile(k_lo, N_GROUPS), jnp.int32)
    hi_table = jnp.asarray(group * EXPERTS_PER_GROUP + np.tile(k_hi, N_GROUPS), jnp.int32)
    i32 = jnp.int32
    return {"first_tile": first_tile.astype(i32), "tiles": tiles_c.astype(i32),
            "total": total.astype(i32).reshape(1), "lo": lo_table[tile_cls],
            "hi": hi_table[tile_cls], "valid": (t < total).astype(i32), "n_tiles": n_tiles,
            "tile_rows": tile_rows}


def _final_kernel(x_ref, y_ref, gate2_ref, fn_ref, o_ref):
    o_ref[...] = _rms(x_ref[...] + gate2_ref[0] * _load_token_major(y_ref)) * fn_ref[...]


def _final(x2, y, mod, final_norm, seq):
    n = x2.shape[0]
    tm = min(ROW_TILE, seq)
    tps = seq // tm
    row = pl.BlockSpec((tm, D_MODEL), lambda i: (i, 0))
    return pl.pallas_call(
        _final_kernel,
        grid=(n // tm,),
        in_specs=[row, pl.BlockSpec((tm * ROW_CHUNKS, LANES), lambda i: (i, 0)),
                  pl.BlockSpec((1, 1, D_MODEL), lambda i: (N_MOD * (i // tps) + 5, 0, 0)),
                  pl.BlockSpec(final_norm.shape, lambda i: (0, 0))],
        out_specs=row,
        out_shape=jax.ShapeDtypeStruct((n, D_MODEL), F32),
        compiler_params=_params("arbitrary"),
        name="final_norm",
    )(x2, y, mod, final_norm)


def _rope_tables(seq):
    rows = seq // GRID_W
    row = jnp.repeat(jnp.arange(rows, dtype=F32), GRID_W)
    col = jnp.tile(jnp.arange(GRID_W, dtype=F32), rows)
    inv = ROPE_THETA ** (-jnp.arange(ROT_FREQS, dtype=F32) / ROT_FREQS)
    ang = jnp.stack([row[:, None] * inv, col[:, None] * inv], axis=1)
    cos = jnp.cos(ang)
    sin = jnp.sin(ang)
    cos_h = jnp.concatenate([cos, cos], axis=-1).reshape(seq, HEAD_DIM)
    sin_h = jnp.concatenate([-sin, sin], axis=-1).reshape(seq, HEAD_DIM)
    return jnp.tile(cos_h, (1, 2)), jnp.tile(sin_h, (1, 2))


def kernel(x_prompt, x_sample, c_prompt, c_sample, w_ada, b_ada, norm_attn, norm_ffn, w_in,
           q_norm_a, k_norm_a, lambda_q1, lambda_k1, lambda_q2, lambda_k2, subln_b, rel_bias,
           w_branch_a, w_branch_b, w_out, w_router, router_bias, w_gate, w_up, w_down,
           final_norm):
    depth = w_in.shape[0]
    w_in_b = w_in.astype(BF16)
    wa_b = w_branch_a.astype(BF16)
    wb_b = w_branch_b.astype(BF16)
    wo_b = w_out.astype(BF16)
    wg_b = w_gate.astype(BF16).reshape(-1, D_MODEL, D_FF_EXPERT)
    wu_b = w_up.astype(BF16).reshape(-1, D_MODEL, D_FF_EXPERT)
    wd_b = w_down.astype(BF16).reshape(-1, D_FF_EXPERT, D_MODEL)
    wr_hi = w_router.astype(BF16)
    wr_lo = (w_router - wr_hi.astype(F32)).astype(BF16)
    lane_pad = ((0, 0), (0, LANES - N_EXPERTS))
    wr_pad = jnp.concatenate([jnp.pad(wr_hi, lane_pad), jnp.pad(wr_lo, lane_pad)], axis=1)
    wr_t = w_router.T
    fn = final_norm.reshape(1, -1)

    def run(x3, c):
        bsz, seq, _ = x3.shape
        tq = min(Q_TILE, seq)
        cos_t, sin_t = _rope_tables(seq)
        bias_t = _bias_table(rel_bias, seq, tq)
        n = bsz * seq
        x = x3.reshape(n, D_MODEL)
        sub_rows = min(ROW_TILE, seq) // MERGE_SUBTILES
        tri = jnp.asarray(np.triu(np.ones((sub_rows, sub_rows), np.float32), 1), BF16)
        delta = None
        prev_mod = None
        for l in range(depth):
            mod = _ada(c, w_ada, b_ada, l).reshape(bsz * N_MOD, 1, D_MODEL)
            outs = _inproj(
                x, delta, mod, prev_mod, norm_attn[l].reshape(1, -1), w_in_b[l],
                jnp.tile(q_norm_a[l], 2).reshape(1, -1), jnp.tile(k_norm_a[l], 2).reshape(1, -1),
                cos_t, sin_t, seq)
            qa, ka, va, qb, kb, vb, g = outs[:7]
            if delta is not None:
                x = outs[7]
            oa = _gqa(qa, ka, va, bsz, seq)
            lam_init = 0.8 - 0.6 * math.exp(-0.3 * l)
            ob = _diff(qb, kb, vb, bias_t, lambda_q1[l].reshape(1, -1), lambda_k1[l].reshape(1, -1),
                       lambda_q2[l].reshape(1, -1), lambda_k2[l].reshape(1, -1),
                       subln_b[l].reshape(1, -1), lam_init, bsz, seq)
            x, hx, cls, rank, counts = _merge(oa, ob, g, x, mod, norm_ffn[l].reshape(1, -1),
                                              wa_b[l], wb_b[l], wo_b[l], wr_pad, tri, router_bias,
                                              seq)
            cls = cls.reshape(n)
            rank = rank.reshape(n)
            plan = _tile_plan(counts[:N_CLASSES, 0].astype(jnp.int32), n)
            n_slots = plan["n_tiles"] * plan["tile_rows"]
            slot = (plan["first_tile"] * plan["tile_rows"])[cls] + rank
            hs = _dispatch(hx.reshape(n, ROW_CHUNKS, LANES), slot, plan)
            ys = _moe(hs.reshape(n_slots * ROW_CHUNKS, LANES), plan["tile_rows"],
                      plan["lo"] + l * N_EXPERTS, plan["hi"] + l * N_EXPERTS, plan["valid"], wr_t,
                      wg_b, wu_b, wd_b)
            delta = _combine(ys.reshape(n_slots, ROW_CHUNKS, LANES), slot, n)
            delta = delta.reshape(n * ROW_CHUNKS, LANES)
            prev_mod = mod
        return _final(x, delta, prev_mod, fn, seq).reshape(bsz, seq, D_MODEL)

    return (run(x_prompt, c_prompt), run(x_sample, c_sample))
```
